```python
import math
import jax, jax.numpy as jnp
from jax import lax
import numpy as np

D_MODEL = 1024
BATCH = 4
SEQ = 8192
DEPTH = 2

D_FF = 2816
GDN_HEADS = 4
GDN_HEAD_DIM = 128
GDN_CONV = 4
GDN_CHUNK = 64
GDN_KEY = GDN_HEADS * GDN_HEAD_DIM
GDN_VAL = GDN_HEADS * GDN_HEAD_DIM
SC_GROUPS = 4
SC_CHANNELS = 512
SC_CONV = 3
HYB_IN = 2 * GDN_KEY + 2 * GDN_VAL + 2 * GDN_HEADS + 3 * SC_CHANNELS
HYB_MIX = GDN_VAL + SC_CHANNELS
SWA_HEADS = 16
SWA_KV_HEADS = 4
SWA_HEAD_DIM = 64
SWA_WINDOW = 128
SWA_QKV = (SWA_HEADS + 2 * SWA_KV_HEADS) * SWA_HEAD_DIM
SWA_OUT = SWA_HEADS * SWA_HEAD_DIM
ROPE_THETA = 10000.0
RMS_EPS = 1e-6
L2_EPS = 1e-6

kernel_name = "hybrid_gdn_shortconv_swa_macaron"


def rmsnorm(x, g):
    xf = x.astype(jnp.float32)
    y = xf * lax.rsqrt(jnp.mean(xf * xf, axis=-1, keepdims=True) + RMS_EPS)
    return (y * g.astype(jnp.float32)).astype(x.dtype)


def swiglu(x, w_gate, w_up, w_down):
    return (jax.nn.silu(x @ w_gate) * (x @ w_up)) @ w_down


def causal_dwconv(x, w):
    width, ch = w.shape
    return lax.conv_general_dilated(
        x, w[:, None, :].astype(x.dtype), window_strides=(1,), padding=[(width - 1, 0)],
        dimension_numbers=("NWC", "WIO", "NWC"), feature_group_count=ch)


def l2norm(x):
    return x * lax.rsqrt(jnp.sum(x * x, axis=-1, keepdims=True) + L2_EPS)


def gated_delta_chunked(q, k, v, beta, g):
    b_, t_, h_, dk = q.shape
    dv = v.shape[-1]
    c_ = GDN_CHUNK
    n_ = t_ // c_

    def to_chunks(a):
        a = a.reshape((b_, n_, c_, h_) + a.shape[3:])
        return jnp.moveaxis(a, 3, 1)

    q, k, v, beta, g = (to_chunks(a) for a in (q, k, v, beta, g))
    q = q * (dk ** -0.5)
    g_cum = jnp.cumsum(g, axis=-1)
    idx = jnp.arange(c_)
    lower_incl = idx[:, None] >= idx[None, :]
    decay = jnp.exp(jnp.where(lower_incl, g_cum[..., :, None] - g_cum[..., None, :], -jnp.inf))
    k_beta = k * beta[..., None]
    v_beta = v * beta[..., None]
    m = jnp.einsum("bhncd,bhnsd->bhncs", k_beta, k) * decay
    rhs = jnp.concatenate([v_beta, k_beta * jnp.exp(g_cum)[..., None]], axis=-1)
    sol = lax.linalg.triangular_solve(m, rhs, left_side=True, lower=True, unit_diagonal=True)
    u, w = sol[..., :dv], sol[..., dv:]
    attn_intra = jnp.einsum("bhncd,bhnsd->bhncs", q, k) * decay
    q_g = q * jnp.exp(g_cum)[..., None]
    g_last = g_cum[..., -1]
    k_g = k * jnp.exp(g_last[..., None] - g_cum)[..., None]
    xs = tuple(jnp.moveaxis(a, 2, 0) for a in (q_g, w, u, attn_intra, k_g, jnp.exp(g_last)))

    def step(state, inp):
        qg, wc, uc, a, kg, dl = inp
        v_new = uc - jnp.einsum("bhcd,bhde->bhce", wc, state)
        o = jnp.einsum("bhcd,bhde->bhce", qg, state) + jnp.einsum("bhcs,bhse->bhce", a, v_new)
        state = state * dl[..., None, None] + jnp.einsum("bhcd,bhce->bhde", kg, v_new)
        return state, o

    s0 = jnp.zeros((b_, h_, dk, dv), jnp.float32)
    _, o = lax.scan(step, s0, xs)
    o = jnp.moveaxis(jnp.moveaxis(o, 0, 2), 1, 3)
    return o.reshape(b_, t_, h_, dv)


def hybrid_gdn_shortconv(h, w_in, gdn_conv_w, a_log, dt_bias, out_norm, sc_conv_w, w_out):
    b_, t_, _ = h.shape
    p = h @ w_in
    sizes = [GDN_KEY, GDN_KEY, GDN_VAL, GDN_VAL, GDN_HEADS, GDN_HEADS,
             SC_CHANNELS, SC_CHANNELS, SC_CHANNELS]
    q, k, v, z, b_raw, a_raw, sc_b, sc_c, sc_h = jnp.split(p, np.cumsum(sizes)[:-1].tolist(), axis=-1)
    qkv = jax.nn.silu(causal_dwconv(jnp.concatenate([q, k, v], axis=-1), gdn_conv_w)).astype(jnp.float32)
    q, k, v = jnp.split(qkv, [GDN_KEY, 2 * GDN_KEY], axis=-1)
    q = l2norm(q.reshape(b_, t_, GDN_HEADS, GDN_HEAD_DIM))
    k = l2norm(k.reshape(b_, t_, GDN_HEADS, GDN_HEAD_DIM))
    v = v.reshape(b_, t_, GDN_HEADS, GDN_HEAD_DIM)
    beta = jax.nn.sigmoid(b_raw.astype(jnp.float32))
    g = -jnp.exp(a_log.astype(jnp.float32)) * jax.nn.softplus(
        a_raw.astype(jnp.float32) + dt_bias.astype(jnp.float32))
    o = gated_delta_chunked(q, k, v, beta, g)
    o = o * lax.rsqrt(jnp.mean(o * o, axis=-1, keepdims=True) + RMS_EPS)
    zf = z.astype(jnp.float32).reshape(b_, t_, GDN_HEADS, GDN_HEAD_DIM)
    o = (o * out_norm.astype(jnp.float32) * jax.nn.silu(zf)).reshape(b_, t_, GDN_VAL).astype(h.dtype)
    y_sc = sc_b * causal_dwconv(sc_c * sc_h, sc_conv_w)
    return jnp.concatenate([o, y_sc], axis=-1) @ w_out


def rope(x, pos):
    half = x.shape[-1] // 2
    inv = ROPE_THETA ** (-jnp.arange(half, dtype=jnp.float32) / half)
    ang = pos.astype(jnp.float32)[:, None] * inv[None, :]
    cos, sin = jnp.cos(ang)[None, :, None, :], jnp.sin(ang)[None, :, None, :]
    xf = x.astype(jnp.float32)
    x1, x2 = xf[..., :half], xf[..., half:]
    return jnp.concatenate([x1 * cos - x2 * sin, x2 * cos + x1 * sin], axis=-1).astype(x.dtype)


def sliding_window_attention(h, w_qkv, b_qkv, sinks, w_o, b_o):
    b_, t_, _ = h.shape
    grp = SWA_HEADS // SWA_KV_HEADS
    win = SWA_WINDOW
    nb = t_ // win
    d = SWA_HEAD_DIM
    qkv = h @ w_qkv + b_qkv
    q, k, v = jnp.split(qkv, [SWA_HEADS * d, (SWA_HEADS + SWA_KV_HEADS) * d], axis=-1)
    pos = jnp.arange(t_)
    q = rope(q.reshape(b_, t_, SWA_HEADS, d), pos)
    k = rope(k.reshape(b_, t_, SWA_KV_HEADS, d), pos)
    v = v.reshape(b_, t_, SWA_KV_HEADS, d)
    qb = q.reshape(b_, nb, win, SWA_KV_HEADS, grp, d)

    def band(a):
        ap = jnp.pad(a, ((0, 0), (win, 0), (0, 0), (0, 0))).reshape(b_, nb + 1, win, SWA_KV_HEADS, d)
        return jnp.concatenate([ap[:, :-1], ap[:, 1:]], axis=2)

    kb, vb = band(k), band(v)
    s = jnp.einsum("bnqhgd,bnkhd->bnhgqk", qb, kb, preferred_element_type=jnp.float32) * (d ** -0.5)
    qi = jnp.arange(win)[:, None]
    kj = jnp.arange(2 * win)[None, :]
    in_window = (kj > qi) & (kj <= qi + win)
    valid = in_window[None] & ((jnp.arange(nb)[:, None, None] > 0) | (kj[None] >= win))
    s = jnp.where(valid[None, :, None, None], s, -jnp.inf)
    sink = sinks.astype(jnp.float32).reshape(1, 1, SWA_KV_HEADS, grp, 1, 1)
    mx = jnp.maximum(jnp.max(s, axis=-1, keepdims=True), sink)
    e = jnp.exp(s - mx)
    p = e / (jnp.sum(e, axis=-1, keepdims=True) + jnp.exp(sink - mx))
    o = jnp.einsum("bnhgqk,bnkhd->bnqhgd", p.astype(vb.dtype), vb)
    return o.reshape(b_, t_, SWA_OUT) @ w_o + b_o


def setup_inputs(seed: int = 0) -> dict:
    key = jax.random.key(seed)
    ks = iter(jax.random.split(key, 40))
    f32 = jnp.float32

    def nrm(shape, scale):
        return scale * jax.random.normal(next(ks), shape, f32)

    ne, no = (DEPTH + 1) // 2, DEPTH // 2
    D, F = D_MODEL, D_FF
    x = nrm((BATCH, SEQ, D), 1.0)
    ffn1_norm = 1.0 + nrm((DEPTH, D), 0.02)
    ffn1_w_gate = nrm((DEPTH, D, F), D ** -0.5)
    ffn1_w_up = nrm((DEPTH, D, F), D ** -0.5)
    ffn1_w_down = nrm((DEPTH, F, D), F ** -0.5)
    mix_norm = 1.0 + nrm((DEPTH, D), 0.02)
    ffn2_norm = 1.0 + nrm((DEPTH, D), 0.02)
    ffn2_w_gate = nrm((DEPTH, D, F), D ** -0.5)
    ffn2_w_up = nrm((DEPTH, D, F), D ** -0.5)
    ffn2_w_down = nrm((DEPTH, F, D), F ** -0.5)
    hyb_w_in = nrm((ne, D, HYB_IN), D ** -0.5)
    gdn_conv_w = nrm((ne, GDN_CONV, 2 * GDN_KEY + GDN_VAL), GDN_CONV ** -0.5)
    gdn_a_log = jnp.log(jax.random.uniform(next(ks), (ne, GDN_HEADS), f32, 1.0, 16.0))
    dt = jnp.exp(jax.random.uniform(next(ks), (ne, GDN_HEADS), f32, math.log(1e-3), math.log(1e-1)))
    gdn_dt_bias = dt + jnp.log(-jnp.expm1(-dt))
    gdn_out_norm = 1.0 + nrm((ne, GDN_HEAD_DIM), 0.02)
    sc_conv_w = nrm((ne, SC_CONV, SC_CHANNELS), SC_CONV ** -0.5)
    hyb_w_out = nrm((ne, HYB_MIX, D), HYB_MIX ** -0.5)
    swa_w_qkv = nrm((no, D, SWA_QKV), D ** -0.5)
    swa_b_qkv = nrm((no, SWA_QKV), 0.02)
    swa_sinks = nrm((no, SWA_HEADS), 0.5)
    swa_w_o = nrm((no, SWA_OUT, D), SWA_OUT ** -0.5)
    swa_b_o = nrm((no, D), 0.02)
    final_norm = 1.0 + nrm((D,), 0.02)
    return {"x": x,
            "ffn1_norm": ffn1_norm, "ffn1_w_gate": ffn1_w_gate, "ffn1_w_up": ffn1_w_up, "ffn1_w_down": ffn1_w_down,
            "mix_norm": mix_norm,
            "ffn2_norm": ffn2_norm, "ffn2_w_gate": ffn2_w_gate, "ffn2_w_up": ffn2_w_up, "ffn2_w_down": ffn2_w_down,
            "hyb_w_in": hyb_w_in, "gdn_conv_w": gdn_conv_w, "gdn_a_log": gdn_a_log, "gdn_dt_bias": gdn_dt_bias,
            "gdn_out_norm": gdn_out_norm, "sc_conv_w": sc_conv_w, "hyb_w_out": hyb_w_out,
            "swa_w_qkv": swa_w_qkv, "swa_b_qkv": swa_b_qkv, "swa_sinks": swa_sinks, "swa_w_o": swa_w_o,
            "swa_b_o": swa_b_o, "final_norm": final_norm}


def reference(x, ffn1_norm, ffn1_w_gate, ffn1_w_up, ffn1_w_down, mix_norm,
              ffn2_norm, ffn2_w_gate, ffn2_w_up, ffn2_w_down,
              hyb_w_in, gdn_conv_w, gdn_a_log, gdn_dt_bias, gdn_out_norm, sc_conv_w, hyb_w_out,
              swa_w_qkv, swa_b_qkv, swa_sinks, swa_w_o, swa_b_o, final_norm):
    for li in range(DEPTH):
        x = x + 0.5 * swiglu(rmsnorm(x, ffn1_norm[li]), ffn1_w_gate[li], ffn1_w_up[li], ffn1_w_down[li])
        h = rmsnorm(x, mix_norm[li])
        j = li // 2
        if li % 2 == 0:
            x = x + hybrid_gdn_shortconv(h, hyb_w_in[j], gdn_conv_w[j], gdn_a_log[j], gdn_dt_bias[j],
                                         gdn_out_norm[j], sc_conv_w[j], hyb_w_out[j])
        else:
            x = x + sliding_window_attention(h, swa_w_qkv[j], swa_b_qkv[j], swa_sinks[j],
                                             swa_w_o[j], swa_b_o[j])
        x = x + 0.5 * swiglu(rmsnorm(x, ffn2_norm[li]), ffn2_w_gate[li], ffn2_w_up[li], ffn2_w_down[li])
    return rmsnorm(x, final_norm)
```

```python
import functools

import jax
import jax.numpy as jnp
from jax import lax
from jax.experimental import pallas as pl
from jax.experimental.pallas import tpu as pltpu

F32 = jnp.float32
BF16 = jnp.bfloat16

D_MODEL = 1024
D_FF = 2816
GDN_HEADS = 4
GDN_HEAD_DIM = 128
GDN_CONV = 4
GDN_QKV = 3 * GDN_HEADS * GDN_HEAD_DIM
GDN_Z = GDN_HEADS * GDN_HEAD_DIM
SC_CHANNELS = 512
SC_CONV = 3
SWA_HEADS = 16
SWA_KV_HEADS = 4
SWA_HEAD_DIM = 64
SWA_WINDOW = 128
SWA_Q = SWA_HEADS * SWA_HEAD_DIM
SWA_KV = SWA_KV_HEADS * SWA_HEAD_DIM
ROPE_THETA = 10000.0
RMS_EPS = 1e-6
L2_EPS = 1e-6

LANES = 128
SUBLANES = 8
VMEM_LIMIT_BYTES = 56 * 1024 * 1024

FFN_ROWS = 512
FFN_COLS = 256
HYB_ROWS = 256
GDN_UNIT = 128
SWA_ROWS = 512
CONV_HALO = SUBLANES

NEG_INF = float("-inf")


def _rms(x, g):
    ms = jnp.mean(x * x, axis=-1, keepdims=True)
    return x * lax.rsqrt(ms + RMS_EPS) * g


def _sigmoid(x):
    return 1.0 / (1.0 + jnp.exp(-x))


def _silu(x):
    return x * _sigmoid(x)


def _dot(a, b):
    return jnp.dot(a, b, preferred_element_type=F32)


def _dot_nt(a, b):
    return lax.dot_general(a, b, (((1,), (1,)), ((), ())), preferred_element_type=F32)


def _dot_tn(a, b):
    return lax.dot_general(a, b, (((0,), (0,)), ((), ())), preferred_element_type=F32)


def _resident(shape):
    nd = len(shape)
    return pl.BlockSpec(shape, lambda *_: (0,) * nd, pipeline_mode=pl.Buffered(1))


def _ffn_body(x_ref, g_ref, wg_ref, wu_ref, wd_ref, fg_ref, o_ref, act_ref, *, final_norm):
    x = x_ref[...]
    h = _rms(x, g_ref[...]).astype(BF16)
    for c in range(D_FF // FFN_COLS):
        cols = slice(c * FFN_COLS, (c + 1) * FFN_COLS)
        gate = _dot(h, wg_ref[:, cols])
        up = _dot(h, wu_ref[:, cols])
        act_ref[:, cols] = (_silu(gate) * up).astype(BF16)
    y = x + 0.5 * _dot(act_ref[...], wd_ref[...])
    if final_norm:
        y = _rms(y, fg_ref[...])
    o_ref[...] = y


def _ffn(x2d, norm_g, wg, wu, wd, final_g, final_norm):
    rows = x2d.shape[0]
    assert rows % FFN_ROWS == 0
    row_spec = pl.BlockSpec((FFN_ROWS, D_MODEL), lambda i: (i, 0))
    return pl.pallas_call(
        functools.partial(_ffn_body, final_norm=final_norm),
        grid=(rows // FFN_ROWS,),
        in_specs=[row_spec, _resident((1, D_MODEL)), _resident((D_MODEL, D_FF)), _resident((D_MODEL, D_FF)),
                  _resident((D_FF, D_MODEL)), _resident((1, D_MODEL))],
        out_specs=row_spec,
        out_shape=jax.ShapeDtypeStruct((rows, D_MODEL), F32),
        scratch_shapes=[pltpu.VMEM((FFN_ROWS, D_FF), BF16)],
        compiler_params=pltpu.CompilerParams(dimension_semantics=("arbitrary",),
                                             vmem_limit_bytes=VMEM_LIMIT_BYTES),
        name="ffn_final" if final_norm else "ffn",
    )(x2d, norm_g, wg, wu, wd, final_g)


def _causal_conv(ext_ref, cur, w_ref, width, first):
    rows = cur.shape[0]

    @pl.when(first)
    def _():
        ext_ref[0:CONV_HALO, :] = jnp.zeros((CONV_HALO, cur.shape[1]), F32)

    ext_ref[CONV_HALO:CONV_HALO + rows, :] = cur
    acc = cur * w_ref[width - 1:width, :]
    for j in range(width - 1):
        back = width - 1 - j
        acc = acc + ext_ref[CONV_HALO - back:CONV_HALO - back + rows, :] * w_ref[j:j + 1, :]
    ext_ref[0:CONV_HALO, :] = ext_ref[rows:rows + CONV_HALO, :]
    return acc


def _segment_cumsum(x, seg):
    rows = x.shape[0]
    pos = lax.broadcasted_iota(jnp.int32, x.shape, 0) % seg
    shift = 1
    while shift < seg:
        x = x + jnp.where(pos >= shift, pltpu.roll(x, shift, axis=0), 0.0)
        shift *= 2
    del rows
    return x


def _l2norm_heads(x, heads, dim):
    outs = []
    for h in range(heads):
        xh = x[:, h * dim:(h + 1) * dim]
        outs.append(xh * lax.rsqrt(jnp.sum(xh * xh, axis=-1, keepdims=True) + L2_EPS))
    return outs


def _unit_lower_inverse(n, size):
    row = lax.broadcasted_iota(jnp.int32, (size, size), 0)
    col = lax.broadcasted_iota(jnp.int32, (size, size), 1)
    inv = jnp.where(row == col, 1.0, 0.0) + n
    power = n
    span = 2
    while span < size:
        pb = power.astype(BF16)
        power = _dot(pb, pb)
        inv = inv + _dot(inv.astype(BF16), power.astype(BF16))
        span *= 2
    return inv


def _hyb_body(x_ref, g_ref, wqkvz_ref, wsc_ref, wba_ref, convw_ref, alog_ref, dtb_ref, onorm_ref, scw_ref,
              wout_ref, o_ref, ext_ref, ext2_ref, state_ref):
    first = pl.program_id(1) == 0
    rows = x_ref.shape[1]
    hd = GDN_HEAD_DIM
    x = x_ref[0]
    h = _rms(x, g_ref[...]).astype(BF16)

    @pl.when(first)
    def _():
        state_ref[...] = jnp.zeros(state_ref.shape, F32)

    p_qkvz = _dot(h, wqkvz_ref[...])
    p_ba = _dot(h, wba_ref[...])
    p_sc = _dot(h, wsc_ref[...])

    qkv = _silu(_causal_conv(ext_ref, p_qkvz[:, :GDN_QKV], convw_ref, GDN_CONV, first))
    z = p_qkvz[:, GDN_QKV:]
    qs = _l2norm_heads(qkv[:, 0:GDN_Z], GDN_HEADS, hd)
    ks = _l2norm_heads(qkv[:, GDN_Z:2 * GDN_Z], GDN_HEADS, hd)
    beta_all = _sigmoid(p_ba)
    sp_in = p_ba + dtb_ref[...]
    softplus = jnp.maximum(sp_in, 0.0) + jnp.log1p(jnp.exp(-jnp.abs(sp_in)))
    g_all = -jnp.exp(alog_ref[...]) * softplus
    gc_all = _segment_cumsum(g_all, GDN_UNIT)
    gc_rows = gc_all.T

    row = lax.broadcasted_iota(jnp.int32, (GDN_UNIT, GDN_UNIT), 0)
    col = lax.broadcasted_iota(jnp.int32, (GDN_UNIT, GDN_UNIT), 1)
    lower_incl = row >= col
    lower_strict = row > col

    gdn_out = []
    for u in range(rows // GDN_UNIT):
        r0 = u * GDN_UNIT
        rs = slice(r0, r0 + GDN_UNIT)
        head_out = []
        for hh in range(GDN_HEADS):
            q = qs[hh][rs] * (hd ** -0.5)
            k = ks[hh][rs]
            v = qkv[rs, 2 * GDN_Z + hh * hd:2 * GDN_Z + (hh + 1) * hd]
            beta = beta_all[rs, hh:hh + 1]
            gc = gc_all[rs, GDN_HEADS + hh:GDN_HEADS + hh + 1]
            gc_row = gc_rows[GDN_HEADS + hh:GDN_HEADS + hh + 1, rs]
            g_last = gc[GDN_UNIT - 1:GDN_UNIT, :]
            decay = jnp.exp(jnp.where(lower_incl, gc - gc_row, NEG_INF))
            e_gc = jnp.exp(gc)
            kb = k * beta
            kb16 = kb.astype(BF16)
            k16 = k.astype(BF16)
            kk = _dot_nt(kb16, k16)
            attn = _dot_nt(q.astype(BF16), k16) * decay
            neg_m = jnp.where(lower_strict, -(kk * decay), 0.0)
            t_inv = _unit_lower_inverse(neg_m, GDN_UNIT)
            rhs = jnp.concatenate([(v * beta).astype(BF16), (kb * e_gc).astype(BF16)], axis=-1)
            sol = _dot(t_inv.astype(BF16), rhs)
            u_mat, w_mat = sol[:, :hd], sol[:, hd:]
            qg = q * e_gc
            kg = k * jnp.exp(g_last - gc)
            state = state_ref[hh]
            s16 = state.astype(BF16)
            v_new = u_mat - _dot(w_mat.astype(BF16), s16)
            v16 = v_new.astype(BF16)
            o = _dot(qg.astype(BF16), s16) + _dot(attn.astype(BF16), v16)
            state_ref[hh] = state * jnp.exp(g_last) + _dot_tn(kg.astype(BF16), v16)
            o = o * lax.rsqrt(jnp.mean(o * o, axis=-1, keepdims=True) + RMS_EPS)
            o = o * onorm_ref[...] * _silu(z[rs, hh * hd:(hh + 1) * hd])
            head_out.append(o)
        gdn_out.append(jnp.concatenate(head_out, axis=-1))
    gdn = jnp.concatenate(gdn_out, axis=0)

    sc_b = p_sc[:, 0:SC_CHANNELS]
    sc_ch = p_sc[:, SC_CHANNELS:2 * SC_CHANNELS] * p_sc[:, 2 * SC_CHANNELS:3 * SC_CHANNELS]
    y_sc = sc_b * _causal_conv(ext2_ref, sc_ch, scw_ref, SC_CONV, first)

    mix = jnp.concatenate([gdn, y_sc], axis=-1).astype(BF16)
    o_ref[0] = x + _dot(mix, wout_ref[...])


def _hybrid(x, norm_g, w_in, conv_w, a_log, dt_bias, out_norm, sc_conv_w, w_out):
    b, t, _ = x.shape
    assert t % HYB_ROWS == 0 and HYB_ROWS % GDN_UNIT == 0
    n_qkvz = GDN_QKV + GDN_Z
    w_qkvz = w_in[:, :n_qkvz].astype(BF16)
    w_ba = jnp.pad(w_in[:, n_qkvz:n_qkvz + 2 * GDN_HEADS], ((0, 0), (0, LANES - 2 * GDN_HEADS))).astype(BF16)
    w_sc = w_in[:, n_qkvz + 2 * GDN_HEADS:].astype(BF16)
    lane_pad = (GDN_HEADS, LANES - 2 * GDN_HEADS)
    alog_row = jnp.pad(a_log.astype(F32), lane_pad).reshape(1, LANES)
    dtb_row = jnp.pad(dt_bias.astype(F32), lane_pad).reshape(1, LANES)
    tile = pl.BlockSpec((1, HYB_ROWS, D_MODEL), lambda bi, ti: (bi, ti, 0))
    return pl.pallas_call(
        _hyb_body,
        grid=(b, t // HYB_ROWS),
        in_specs=[tile, _resident((1, D_MODEL)), _resident((D_MODEL, n_qkvz)), _resident((D_MODEL, 3 * SC_CHANNELS)),
                  _resident((D_MODEL, LANES)), _resident((GDN_CONV, GDN_QKV)), _resident((1, LANES)),
                  _resident((1, LANES)), _resident((1, GDN_HEAD_DIM)), _resident((SC_CONV, SC_CHANNELS)),
                  _resident((GDN_Z + SC_CHANNELS, D_MODEL))],
        out_specs=tile,
        out_shape=jax.ShapeDtypeStruct(x.shape, F32),
        scratch_shapes=[pltpu.VMEM((CONV_HALO + HYB_ROWS, GDN_QKV), F32),
                        pltpu.VMEM((CONV_HALO + HYB_ROWS, SC_CHANNELS), F32),
                        pltpu.VMEM((GDN_HEADS, GDN_HEAD_DIM, GDN_HEAD_DIM), F32)],
        compiler_params=pltpu.CompilerParams(dimension_semantics=("arbitrary", "arbitrary"),
                                             vmem_limit_bytes=VMEM_LIMIT_BYTES),
        name="hybrid_mixer",
    )(x, norm_g, w_qkvz, w_sc, w_ba, conv_w.astype(F32), alog_row, dtb_row,
      out_norm.reshape(1, GDN_HEAD_DIM).astype(F32), sc_conv_w.astype(F32), w_out.astype(BF16))


def _rope_body(inv_ref, cos_ref, sin_ref):
    rows = cos_ref.shape[0]
    half = SWA_HEAD_DIM // 2
    t = lax.broadcasted_iota(jnp.int32, (rows, LANES), 0) + pl.program_id(0) * rows
    lane = lax.broadcasted_iota(jnp.int32, (rows, LANES), 1)
    ang = t.astype(F32) * inv_ref[...]
    cos_ref[...] = jnp.cos(ang)
    sin_ref[...] = jnp.where(lane % SWA_HEAD_DIM < half, -1.0, 1.0) * jnp.sin(ang)


def _rope_tables(t):
    half = SWA_HEAD_DIM // 2
    inv = ROPE_THETA ** (-jnp.arange(half, dtype=F32) / half)
    inv_row = jnp.tile(inv, LANES // half).reshape(1, LANES)
    rows = min(t, 1024)
    assert t % rows == 0
    spec = pl.BlockSpec((rows, LANES), lambda i: (i, 0))
    return pl.pallas_call(
        _rope_body,
        grid=(t // rows,),
        in_specs=[pl.BlockSpec((1, LANES), lambda i: (0, 0))],
        out_specs=[spec, spec],
        out_shape=[jax.ShapeDtypeStruct((t, LANES), F32)] * 2,
        compiler_params=pltpu.CompilerParams(dimension_semantics=("arbitrary",)),
        name="rope_tables",
    )(inv_row)


def _rope_apply(x, cos, sin_signed):
    half = SWA_HEAD_DIM // 2
    width = x.shape[1]
    lane = lax.broadcasted_iota(jnp.int32, x.shape, 1)
    swapped = jnp.where(lane % SWA_HEAD_DIM < half,
                        pltpu.roll(x, width - half, axis=1),
                        pltpu.roll(x, half, axis=1))
    reps = width // LANES
    cos_w = jnp.concatenate([cos] * reps, axis=1)
    sin_w = jnp.concatenate([sin_signed] * reps, axis=1)
    return x * cos_w + swapped * sin_w


def _swa_body(sinks_ref, x_ref, g_ref, wqkv_ref, bqkv_ref, cos_ref, sin_ref, wo_ref, bo_ref, o_ref,
              k_ref, v_ref, attn_ref):
    first = pl.program_id(1) == 0
    rows = x_ref.shape[1]
    win = SWA_WINDOW
    d = SWA_HEAD_DIM
    grp = SWA_HEADS // SWA_KV_HEADS
    x = x_ref[0]
    h = _rms(x, g_ref[...]).astype(BF16)
    qkv = _dot(h, wqkv_ref[...]) + bqkv_ref[...]
    cos = cos_ref[...]
    sin = sin_ref[...]
    q = (_rope_apply(qkv[:, :SWA_Q], cos, sin) * (d ** -0.5)).astype(BF16)
    k = _rope_apply(qkv[:, SWA_Q:SWA_Q + SWA_KV], cos, sin).astype(BF16)
    v = qkv[:, SWA_Q + SWA_KV:].astype(BF16)

    @pl.when(first)
    def _():
        k_ref[0:win, :] = jnp.zeros((win, SWA_KV), BF16)
        v_ref[0:win, :] = jnp.zeros((win, SWA_KV), BF16)

    k_ref[win:win + rows, :] = k
    v_ref[win:win + rows, :] = v

    lane = lax.broadcasted_iota(jnp.int32, (2 * win, LANES), 1)
    low_half = lane < d
    qi = lax.broadcasted_iota(jnp.int32, (2 * win, 2 * win), 0) % win
    kj = lax.broadcasted_iota(jnp.int32, (2 * win, 2 * win), 1)
    in_window = (kj > qi) & (kj <= qi + win)
    srow = lax.broadcasted_iota(jnp.int32, (2 * win, 1), 0)

    for blk in range(rows // win):
        if blk == 0:
            valid = in_window & ((kj >= win) | jnp.logical_not(first))
        else:
            valid = in_window
        kwin = k_ref[blk * win:(blk + 2) * win, :]
        vwin = v_ref[blk * win:(blk + 2) * win, :]
        for kv in range(SWA_KV_HEADS):
            pair = kv // 2
            kp = kwin[:, pair * LANES:(pair + 1) * LANES]
            vp = vwin[:, pair * LANES:(pair + 1) * LANES]
            kp_sw = jnp.concatenate([kp[:, d:], kp[:, :d]], axis=1)
            vp_sw = jnp.concatenate([vp[:, d:], vp[:, :d]], axis=1)
            zero = jnp.zeros_like(kp)
            if kv % 2 == 0:
                k_lo, k_hi = jnp.where(low_half, kp, zero), jnp.where(low_half, zero, kp_sw)
                v_lo, v_hi = jnp.where(low_half, vp, zero), jnp.where(low_half, zero, vp_sw)
            else:
                k_lo, k_hi = jnp.where(low_half, kp_sw, zero), jnp.where(low_half, zero, kp)
                v_lo, v_hi = jnp.where(low_half, vp_sw, zero), jnp.where(low_half, zero, vp)
            c0 = kv * grp * d
            q2 = jnp.concatenate([q[blk * win:(blk + 1) * win, c0:c0 + LANES],
                                  q[blk * win:(blk + 1) * win, c0 + LANES:c0 + 2 * LANES]], axis=0)
            out = None
            for par, (k_pad, v_pad) in enumerate(((k_lo, v_lo), (k_hi, v_hi))):
                s = jnp.where(valid, _dot_nt(q2, k_pad), NEG_INF)
                sink = jnp.where(srow < win, sinks_ref[kv * grp + par], sinks_ref[kv * grp + 2 + par])
                mx = jnp.maximum(jnp.max(s, axis=-1, keepdims=True), sink)
                e = jnp.exp(s - mx)
                p = e / (jnp.sum(e, axis=-1, keepdims=True) + jnp.exp(sink - mx))
                pv = _dot(p.astype(BF16), v_pad)
                out = pv if out is None else out + pv
            attn_ref[blk * win:(blk + 1) * win, c0:c0 + LANES] = out[:win].astype(BF16)
            attn_ref[blk * win:(blk + 1) * win, c0 + LANES:c0 + 2 * LANES] = out[win:].astype(BF16)

    k_ref[0:win, :] = k_ref[rows:rows + win, :]
    v_ref[0:win, :] = v_ref[rows:rows + win, :]
    o_ref[0] = x + _dot(attn_ref[...], wo_ref[...]) + bo_ref[...]


def _swa(x, norm_g, w_qkv, b_qkv, sinks, w_o, b_o):
    b, t, _ = x.shape
    assert t % SWA_ROWS == 0 and SWA_ROWS % SWA_WINDOW == 0
    cos_t, sin_t = _rope_tables(t)
    n_qkv = SWA_Q + 2 * SWA_KV
    tile = pl.BlockSpec((1, SWA_ROWS, D_MODEL), lambda bi, ti, *_: (bi, ti, 0))
    table = pl.BlockSpec((SWA_ROWS, LANES), lambda bi, ti, *_: (ti, 0))

    def resident(shape):
        nd = len(shape)
        return pl.BlockSpec(shape, lambda *_: (0,) * nd, pipeline_mode=pl.Buffered(1))

    return pl.pallas_call(
        _swa_body,
        grid_spec=pltpu.PrefetchScalarGridSpec(
            num_scalar_prefetch=1,
            grid=(b, t // SWA_ROWS),
            in_specs=[tile, resident((1, D_MODEL)), resident((D_MODEL, n_qkv)), resident((1, n_qkv)),
                      table, table, resident((SWA_Q, D_MODEL)), resident((1, D_MODEL))],
            out_specs=tile,
            scratch_shapes=[pltpu.VMEM((SWA_WINDOW + SWA_ROWS, SWA_KV), BF16),
                            pltpu.VMEM((SWA_WINDOW + SWA_ROWS, SWA_KV), BF16),
                            pltpu.VMEM((SWA_ROWS, SWA_Q), BF16)]),
        out_shape=jax.ShapeDtypeStruct(x.shape, F32),
        compiler_params=pltpu.CompilerParams(dimension_semantics=("arbitrary", "arbitrary"),
                                             vmem_limit_bytes=VMEM_LIMIT_BYTES),
        name="swa_mixer",
    )(sinks.astype(F32), x, norm_g, w_qkv.astype(BF16), b_qkv.reshape(1, n_qkv).astype(F32), cos_t, sin_t,
      w_o.astype(BF16), b_o.reshape(1, D_MODEL).astype(F32))


def kernel(x, ffn1_norm, ffn1_w_gate, ffn1_w_up, ffn1_w_down, mix_norm, ffn2_norm, ffn2_w_gate, ffn2_w_up,
           ffn2_w_down, hyb_w_in, gdn_conv_w, gdn_a_log, gdn_dt_bias, gdn_out_norm, sc_conv_w, hyb_w_out,
           swa_w_qkv, swa_b_qkv, swa_sinks, swa_w_o, swa_b_o, final_norm):
    b, t, d = x.shape
    depth = ffn1_norm.shape[0]
    final_g = final_norm.reshape(1, d).astype(F32)

    def ffn(xin, norm, wg, wu, wd, last):
        y = _ffn(xin.reshape(b * t, d), norm.reshape(1, d).astype(F32), wg.astype(BF16), wu.astype(BF16),
                 wd.astype(BF16), final_g, last)
        return y.reshape(b, t, d)

    for li in range(depth):
        x = ffn(x, ffn1_norm[li], ffn1_w_gate[li], ffn1_w_up[li], ffn1_w_down[li], False)
        norm = mix_norm[li].reshape(1, d).astype(F32)
        j = li // 2
        if li % 2 == 0:
            x = _hybrid(x, norm, hyb_w_in[j], gdn_conv_w[j], gdn_a_log[j], gdn_dt_bias[j], gdn_out_norm[j],
                        sc_conv_w[j], hyb_w_out[j])
        else:
            x = _swa(x, norm, swa_w_qkv[j], swa_b_qkv[j], swa_sinks[j], swa_w_o[j], swa_b_o[j])
        x = ffn(x, ffn2_norm[li], ffn2_w_gate[li], ffn2_w_up[li], ffn2_w_down[li], li == depth - 1)
    return x
```

```python
import functools

import jax
import jax.numpy as jnp
from jax import lax
from jax.experimental import pallas as pl
from jax.experimental.pallas import tpu as pltpu

F32 = jnp.float32
BF16 = jnp.bfloat16

D_MODEL = 1024
D_FF = 2816
GDN_HEADS = 4
GDN_HEAD_DIM = 128
GDN_CONV = 4
GDN_QKV = 3 * GDN_HEADS * GDN_HEAD_DIM
GDN_Z = GDN_HEADS * GDN_HEAD_DIM
SC_CHANNELS = 512
SC_CONV = 3
SWA_HEADS = 16
SWA_KV_HEADS = 4
SWA_HEAD_DIM = 64
SWA_WINDOW = 128
SWA_Q = SWA_HEADS * SWA_HEAD_DIM
SWA_KV = SWA_KV_HEADS * SWA_HEAD_DIM
ROPE_THETA = 10000.0
RMS_EPS = 1e-6
L2_EPS = 1e-6

LANES = 128
SUBLANES = 8
VMEM_LIMIT_BYTES = 56 * 1024 * 1024

FFN_ROWS = 512
FFN_COLS = 256
HYB_ROWS = 512
GDN_UNIT = 128
SWA_ROWS = 512
CONV_HALO = SUBLANES

NEG_INF = float("-inf")


def _rms(x, g):
    ms = jnp.mean(x * x, axis=-1, keepdims=True)
    return x * lax.rsqrt(ms + RMS_EPS) * g


def _sigmoid(x):
    return 1.0 / (1.0 + jnp.exp(-x))


def _silu(x):
    return x * _sigmoid(x)


def _dot(a, b):
    return jnp.dot(a, b, preferred_element_type=F32)


def _dot_nt(a, b):
    return lax.dot_general(a, b, (((1,), (1,)), ((), ())), preferred_element_type=F32)


def _dot_tn(a, b):
    return lax.dot_general(a, b, (((0,), (0,)), ((), ())), preferred_element_type=F32)


def _resident(shape):
    nd = len(shape)
    return pl.BlockSpec(shape, lambda *_: (0,) * nd, pipeline_mode=pl.Buffered(1))


def _ffn_body(x_ref, g_ref, wg_ref, wu_ref, wd_ref, fg_ref, o_ref, act_ref, *, final_norm):
    x = x_ref[...]
    h = _rms(x, g_ref[...]).astype(BF16)
    for c in range(D_FF // FFN_COLS):
        cols = slice(c * FFN_COLS, (c + 1) * FFN_COLS)
        gate = _dot(h, wg_ref[:, cols])
        up = _dot(h, wu_ref[:, cols])
        act_ref[:, cols] = (_silu(gate) * up).astype(BF16)
    y = x + 0.5 * _dot(act_ref[...], wd_ref[...])
    if final_norm:
        y = _rms(y, fg_ref[...])
    o_ref[...] = y


def _ffn(x2d, norm_g, wg, wu, wd, final_g, final_norm):
    rows = x2d.shape[0]
    assert rows % FFN_ROWS == 0
    row_spec = pl.BlockSpec((FFN_ROWS, D_MODEL), lambda i: (i, 0))
    return pl.pallas_call(
        functools.partial(_ffn_body, final_norm=final_norm),
        grid=(rows // FFN_ROWS,),
        in_specs=[row_spec, _resident((1, D_MODEL)), _resident((D_MODEL, D_FF)), _resident((D_MODEL, D_FF)),
                  _resident((D_FF, D_MODEL)), _resident((1, D_MODEL))],
        out_specs=row_spec,
        out_shape=jax.ShapeDtypeStruct((rows, D_MODEL), F32),
        scratch_shapes=[pltpu.VMEM((FFN_ROWS, D_FF), BF16)],
        compiler_params=pltpu.CompilerParams(dimension_semantics=("arbitrary",),
                                             vmem_limit_bytes=VMEM_LIMIT_BYTES),
        name="ffn_final" if final_norm else "ffn",
    )(x2d, norm_g, wg, wu, wd, final_g)


def _causal_conv(ext_ref, cur, w_ref, width):
    rows = cur.shape[0]
    ext_ref[CONV_HALO:CONV_HALO + rows, :] = cur
    acc = cur * w_ref[width - 1:width, :]
    for j in range(width - 1):
        back = width - 1 - j
        acc = acc + ext_ref[CONV_HALO - back:CONV_HALO - back + rows, :] * w_ref[j:j + 1, :]
    ext_ref[0:CONV_HALO, :] = ext_ref[rows:rows + CONV_HALO, :]
    return acc


def _segment_cumsum(x, seg):
    rows = x.shape[0]
    pos = lax.broadcasted_iota(jnp.int32, x.shape, 0) % seg
    shift = 1
    while shift < seg:
        x = x + jnp.where(pos >= shift, pltpu.roll(x, shift, axis=0), 0.0)
        shift *= 2
    del rows
    return x


def _l2norm_heads(x, heads, dim):
    outs = []
    for h in range(heads):
        xh = x[:, h * dim:(h + 1) * dim]
        outs.append(xh * lax.rsqrt(jnp.sum(xh * xh, axis=-1, keepdims=True) + L2_EPS))
    return outs


def _hyb_body(x_ref, g_ref, wqkv_ref, wzsc_ref, wba_ref, convw_ref, alog_ref, dtb_ref, onorm_ref, scw_ref,
              wout_ref, o_ref, ext_ref, ext2_ref, state_ref):
    first = pl.program_id(1) == 0
    rows = x_ref.shape[1]
    hd = GDN_HEAD_DIM
    unit = GDN_UNIT
    x = x_ref[0]
    h = _rms(x, g_ref[...]).astype(BF16)

    @pl.when(first)
    def _():
        state_ref[...] = jnp.zeros(state_ref.shape, F32)
        ext_ref[0:CONV_HALO, :] = jnp.zeros((CONV_HALO, ext_ref.shape[1]), F32)
        ext2_ref[0:CONV_HALO, :] = jnp.zeros((CONV_HALO, ext2_ref.shape[1]), F32)

    p_ba = _dot(h, wba_ref[...])
    p_qkv = _dot(h, wqkv_ref[...])
    p_zsc = _dot(h, wzsc_ref[...])
    z = p_zsc[:, :GDN_Z]

    qkv = _silu(_causal_conv(ext_ref, p_qkv, convw_ref, GDN_CONV))
    qs = _l2norm_heads(qkv[:, 0:GDN_Z], GDN_HEADS, hd)
    ks = _l2norm_heads(qkv[:, GDN_Z:2 * GDN_Z], GDN_HEADS, hd)
    beta_all = _sigmoid(p_ba)
    sp_in = p_ba + dtb_ref[...]
    softplus = jnp.maximum(sp_in, 0.0) + jnp.log1p(jnp.exp(-jnp.abs(sp_in)))
    g_all = -jnp.exp(alog_ref[...]) * softplus
    gc_all = _segment_cumsum(g_all, unit)
    gc_rows = gc_all.T

    row = lax.broadcasted_iota(jnp.int32, (unit, unit), 0)
    col = lax.broadcasted_iota(jnp.int32, (unit, unit), 1)
    lower_incl = row >= col
    lower_strict = row > col
    eye = jnp.where(row == col, 1.0, 0.0)

    n_units = rows // unit
    pairs = [(u, hh) for u in range(n_units) for hh in range(GDN_HEADS)]
    pre = {}
    for p in pairs:
        u, hh = p
        rs = slice(u * unit, (u + 1) * unit)
        q = qs[hh][rs] * (hd ** -0.5)
        k = ks[hh][rs]
        v = qkv[rs, 2 * GDN_Z + hh * hd:2 * GDN_Z + (hh + 1) * hd]
        beta = beta_all[rs, hh:hh + 1]
        gc = gc_all[rs, GDN_HEADS + hh:GDN_HEADS + hh + 1]
        gc_row = gc_rows[GDN_HEADS + hh:GDN_HEADS + hh + 1, rs]
        g_last = gc[unit - 1:unit, :]
        e_gc = jnp.exp(gc)
        kb = k * beta
        pre[p] = dict(
            decay=jnp.exp(jnp.where(lower_incl, gc - gc_row, NEG_INF)),
            kbq16=jnp.concatenate([kb, q], axis=0).astype(BF16),
            k16=k.astype(BF16),
            rhs16=jnp.concatenate([v * beta, kb * e_gc], axis=-1).astype(BF16),
            qg16=(q * e_gc).astype(BF16),
            kg16=(k * jnp.exp(g_last - gc)).astype(BF16),
            s_decay=jnp.exp(g_last))

    kq = {p: _dot_nt(pre[p]["kbq16"], pre[p]["k16"]) for p in pairs}
    neg_m, attn16, inv = {}, {}, {}
    for p in pairs:
        decay = pre[p]["decay"]
        n = jnp.where(lower_strict, -(kq[p][:unit] * decay), 0.0)
        neg_m[p] = n.astype(BF16)
        attn16[p] = (kq[p][unit:] * decay).astype(BF16)
        inv[p] = eye + n
    span = 2
    while span < unit:
        resid = {p: ((eye - inv[p]) + _dot(neg_m[p], inv[p].astype(BF16))).astype(BF16) for p in pairs}
        inv = {p: inv[p] + _dot(inv[p].astype(BF16), resid[p]) for p in pairs}
        span *= 2
    sol = {p: _dot(inv[p].astype(BF16), pre[p]["rhs16"]) for p in pairs}

    gdn_out = []
    for u in range(n_units):
        rs = slice(u * unit, (u + 1) * unit)
        ups = [(u, hh) for hh in range(GDN_HEADS)]
        state = {p: state_ref[p[1]] for p in ups}
        s16 = {p: state[p].astype(BF16) for p in ups}
        wq = {p: _dot(jnp.concatenate([sol[p][:, hd:].astype(BF16), pre[p]["qg16"]], axis=0), s16[p]) for p in ups}
        v16 = {p: (sol[p][:, :hd] - wq[p][:unit]).astype(BF16) for p in ups}
        for p in ups:
            state_ref[p[1]] = state[p] * pre[p]["s_decay"] + _dot_tn(pre[p]["kg16"], v16[p])
        head_out = []
        for p in ups:
            hh = p[1]
            o = wq[p][unit:] + _dot(attn16[p], v16[p])
            o = o * lax.rsqrt(jnp.mean(o * o, axis=-1, keepdims=True) + RMS_EPS)
            head_out.append(o * onorm_ref[...] * _silu(z[rs, hh * hd:(hh + 1) * hd]))
        gdn_out.append(jnp.concatenate(head_out, axis=-1))
    gdn = jnp.concatenate(gdn_out, axis=0)

    sc_b = p_zsc[:, GDN_Z:GDN_Z + SC_CHANNELS]
    sc_ch = (p_zsc[:, GDN_Z + SC_CHANNELS:GDN_Z + 2 * SC_CHANNELS]
             * p_zsc[:, GDN_Z + 2 * SC_CHANNELS:GDN_Z + 3 * SC_CHANNELS])
    y_sc = sc_b * _causal_conv(ext2_ref, sc_ch, scw_ref, SC_CONV)

    mix = jnp.concatenate([gdn, y_sc], axis=-1).astype(BF16)
    o_ref[0] = x + _dot(mix, wout_ref[...])


def _hybrid(x, norm_g, w_in, conv_w, a_log, dt_bias, out_norm, sc_conv_w, w_out):
    b, t, _ = x.shape
    assert t % HYB_ROWS == 0 and HYB_ROWS % GDN_UNIT == 0
    n_qkvz = GDN_QKV + GDN_Z
    n_zsc = GDN_Z + 3 * SC_CHANNELS
    w_qkv = w_in[:, :GDN_QKV].astype(BF16)
    w_ba = jnp.pad(w_in[:, n_qkvz:n_qkvz + 2 * GDN_HEADS], ((0, 0), (0, LANES - 2 * GDN_HEADS))).astype(BF16)
    w_zsc = jnp.concatenate([w_in[:, GDN_QKV:n_qkvz], w_in[:, n_qkvz + 2 * GDN_HEADS:]], axis=1).astype(BF16)
    lane_pad = (GDN_HEADS, LANES - 2 * GDN_HEADS)
    alog_row = jnp.pad(a_log.astype(F32), lane_pad).reshape(1, LANES)
    dtb_row = jnp.pad(dt_bias.astype(F32), lane_pad).reshape(1, LANES)
    tile = pl.BlockSpec((1, HYB_ROWS, D_MODEL), lambda bi, ti: (bi, ti, 0))
    return pl.pallas_call(
        _hyb_body,
        grid=(b, t // HYB_ROWS),
        in_specs=[tile, _resident((1, D_MODEL)), _resident((D_MODEL, GDN_QKV)), _resident((D_MODEL, n_zsc)),
                  _resident((D_MODEL, LANES)), _resident((GDN_CONV, GDN_QKV)), _resident((1, LANES)),
                  _resident((1, LANES)), _resident((1, GDN_HEAD_DIM)), _resident((SC_CONV, SC_CHANNELS)),
                  _resident((GDN_Z + SC_CHANNELS, D_MODEL))],
        out_specs=tile,
        out_shape=jax.ShapeDtypeStruct(x.shape, F32),
        scratch_shapes=[pltpu.VMEM((CONV_HALO + HYB_ROWS, GDN_QKV), F32),
                        pltpu.VMEM((CONV_HALO + HYB_ROWS, SC_CHANNELS), F32),
                        pltpu.VMEM((GDN_HEADS, GDN_HEAD_DIM, GDN_HEAD_DIM), F32)],
        compiler_params=pltpu.CompilerParams(dimension_semantics=("arbitrary", "arbitrary"),
                                             vmem_limit_bytes=VMEM_LIMIT_BYTES),
        name="hybrid_mixer",
    )(x, norm_g, w_qkv, w_zsc, w_ba, conv_w.astype(F32), alog_row, dtb_row,
      out_norm.reshape(1, GDN_HEAD_DIM).astype(F32), sc_conv_w.astype(F32), w_out.astype(BF16))


def _rope_body(inv_ref, cos_ref, sin_ref):
    rows = cos_ref.shape[0]
    half = SWA_HEAD_DIM // 2
    t = lax.broadcasted_iota(jnp.int32, (rows, LANES), 0) + pl.program_id(0) * rows
    lane = lax.broadcasted_iota(jnp.int32, (rows, LANES), 1)
    ang = t.astype(F32) * inv_ref[...]
    cos_ref[...] = jnp.cos(ang)
    sin_ref[...] = jnp.where(lane % SWA_HEAD_DIM < half, -1.0, 1.0) * jnp.sin(ang)


def _rope_tables(t):
    half = SWA_HEAD_DIM // 2
    inv = ROPE_THETA ** (-jnp.arange(half, dtype=F32) / half)
    inv_row = jnp.tile(inv, LANES // half).reshape(1, LANES)
    rows = min(t, 1024)
    assert t % rows == 0
    spec = pl.BlockSpec((rows, LANES), lambda i: (i, 0))
    return pl.pallas_call(
        _rope_body,
        grid=(t // rows,),
        in_specs=[pl.BlockSpec((1, LANES), lambda i: (0, 0))],
        out_specs=[spec, spec],
        out_shape=[jax.ShapeDtypeStruct((t, LANES), F32)] * 2,
        compiler_params=pltpu.CompilerParams(dimension_semantics=("arbitrary",)),
        name="rope_tables",
    )(inv_row)


def _rope_apply(x, cos, sin_signed):
    half = SWA_HEAD_DIM // 2
    width = x.shape[1]
    lane = lax.broadcasted_iota(jnp.int32, x.shape, 1)
    swapped = jnp.where(lane % SWA_HEAD_DIM < half,
                        pltpu.roll(x, width - half, axis=1),
                        pltpu.roll(x, half, axis=1))
    reps = width // LANES
    cos_w = jnp.concatenate([cos] * reps, axis=1)
    sin_w = jnp.concatenate([sin_signed] * reps, axis=1)
    return x * cos_w + swapped * sin_w


def _swa_body(sinks_ref, x_ref, g_ref, wqkv_ref, bqkv_ref, cos_ref, sin_ref, wo_ref, bo_ref, o_ref,
              k_ref, v_ref, attn_ref):
    first = pl.program_id(1) == 0
    rows = x_ref.shape[1]
    win = SWA_WINDOW
    d = SWA_HEAD_DIM
    grp = SWA_HEADS // SWA_KV_HEADS

    @pl.when(first)
    def _():
        k_ref[0:win, :] = jnp.zeros((win, SWA_KV), BF16)
        v_ref[0:win, :] = jnp.zeros((win, SWA_KV), BF16)

    x = x_ref[0]
    h = _rms(x, g_ref[...]).astype(BF16)
    qkv = _dot(h, wqkv_ref[...]) + bqkv_ref[...]
    cos = cos_ref[...]
    sin = sin_ref[...]
    q = (_rope_apply(qkv[:, :SWA_Q], cos, sin) * (d ** -0.5)).astype(BF16)
    k = _rope_apply(qkv[:, SWA_Q:SWA_Q + SWA_KV], cos, sin).astype(BF16)
    v = qkv[:, SWA_Q + SWA_KV:].astype(BF16)

    k_ref[win:win + rows, :] = k
    v_ref[win:win + rows, :] = v
    k_all = k_ref[...]
    v_all = v_ref[...]

    lane = lax.broadcasted_iota(jnp.int32, (win + rows, LANES), 1)
    low_half = lane < d
    k_pad, v_pad = {}, {}
    for pair in range(SWA_KV // LANES):
        for src, dst in ((k_all, k_pad), (v_all, v_pad)):
            col = src[:, pair * LANES:(pair + 1) * LANES]
            swapped = jnp.concatenate([col[:, d:], col[:, :d]], axis=1)
            zero = jnp.zeros_like(col)
            dst[2 * pair, 0] = jnp.where(low_half, col, zero)
            dst[2 * pair, 1] = jnp.where(low_half, zero, swapped)
            dst[2 * pair + 1, 0] = jnp.where(low_half, swapped, zero)
            dst[2 * pair + 1, 1] = jnp.where(low_half, zero, col)

    qi = lax.broadcasted_iota(jnp.int32, (2 * win, 2 * win), 0) % win
    kj = lax.broadcasted_iota(jnp.int32, (2 * win, 2 * win), 1)
    in_window = (kj > qi) & (kj <= qi + win)
    first_valid = in_window & ((kj >= win) | jnp.logical_not(first))
    srow = lax.broadcasted_iota(jnp.int32, (2 * win, 1), 0)
    items = [(kv, par) for kv in range(SWA_KV_HEADS) for par in range(2)]

    def qk_stage(blk):
        out = {}
        for kv, par in items:
            c0 = kv * grp * d
            q2 = jnp.concatenate([q[blk * win:(blk + 1) * win, c0:c0 + LANES],
                                  q[blk * win:(blk + 1) * win, c0 + LANES:c0 + 2 * LANES]], axis=0)
            out[kv, par] = _dot_nt(q2, k_pad[kv, par][blk * win:(blk + 2) * win])
        return out

    n_blk = rows // win
    scores = qk_stage(0)
    for blk in range(n_blk):
        nxt = qk_stage(blk + 1) if blk + 1 < n_blk else None
        valid = first_valid if blk == 0 else in_window
        probs = {}
        for kv, par in items:
            s = jnp.where(valid, scores[kv, par], NEG_INF)
            sink = jnp.where(srow < win, sinks_ref[kv * grp + par], sinks_ref[kv * grp + 2 + par])
            mx = jnp.maximum(jnp.max(s, axis=-1, keepdims=True), sink)
            e = jnp.exp(s - mx)
            inv_den = 1.0 / (jnp.sum(e, axis=-1, keepdims=True) + jnp.exp(sink - mx))
            probs[kv, par] = (e * inv_den).astype(BF16)
        for kv in range(SWA_KV_HEADS):
            c0 = kv * grp * d
            out = (_dot(probs[kv, 0], v_pad[kv, 0][blk * win:(blk + 2) * win])
                   + _dot(probs[kv, 1], v_pad[kv, 1][blk * win:(blk + 2) * win]))
            attn_ref[blk * win:(blk + 1) * win, c0:c0 + LANES] = out[:win].astype(BF16)
            attn_ref[blk * win:(blk + 1) * win, c0 + LANES:c0 + 2 * LANES] = out[win:].astype(BF16)
        scores = nxt

    k_ref[0:win, :] = k_ref[rows:rows + win, :]
    v_ref[0:win, :] = v_ref[rows:rows + win, :]
    o_ref[0] = x + _dot(attn_ref[...], wo_ref[...]) + bo_ref[...]


def _swa(x, norm_g, w_qkv, b_qkv, sinks, w_o, b_o):
    b, t, _ = x.shape
    assert t % SWA_ROWS == 0 and SWA_ROWS % SWA_WINDOW == 0
    cos_t, sin_t = _rope_tables(t)
    n_qkv = SWA_Q + 2 * SWA_KV
    tile = pl.BlockSpec((1, SWA_ROWS, D_MODEL), lambda bi, ti, *_: (bi, ti, 0))
    table = pl.BlockSpec((SWA_ROWS, LANES), lambda bi, ti, *_: (ti, 0))

    def resident(shape):
        nd = len(shape)
        return pl.BlockSpec(shape, lambda *_: (0,) * nd, pipeline_mode=pl.Buffered(1))

    return pl.pallas_call(
        _swa_body,
        grid_spec=pltpu.PrefetchScalarGridSpec(
            num_scalar_prefetch=1,
            grid=(b, t // SWA_ROWS),
            in_specs=[tile, resident((1, D_MODEL)), resident((D_MODEL, n_qkv)), resident((1, n_qkv)),
                      table, table, resident((SWA_Q, D_MODEL)), resident((1, D_MODEL))],
            out_specs=tile,
            scratch_shapes=[pltpu.VMEM((SWA_WINDOW + SWA_ROWS, SWA_KV), BF16),
                            pltpu.VMEM((SWA_WINDOW + SWA_ROWS, SWA_KV), BF16),
                            pltpu.VMEM((SWA_ROWS, SWA_Q), BF16)]),
        out_shape=jax.ShapeDtypeStruct(x.shape, F32),
        compiler_params=pltpu.CompilerParams(dimension_semantics=("arbitrary", "arbitrary"),
                                             vmem_limit_bytes=VMEM_LIMIT_BYTES),
        name="swa_mixer",
    )(sinks.astype(F32), x, norm_g, w_qkv.astype(BF16), b_qkv.reshape(1, n_qkv).astype(F32), cos_t, sin_t,
      w_o.astype(BF16), b_o.reshape(1, D_MODEL).astype(F32))


def kernel(x, ffn1_norm, ffn1_w_gate, ffn1_w_up, ffn1_w_down, mix_norm, ffn2_norm, ffn2_w_gate, ffn2_w_up,
           ffn2_w_down, hyb_w_in, gdn_conv_w, gdn_a_log, gdn_dt_bias, gdn_out_norm, sc_conv_w, hyb_w_out,
           swa_w_qkv, swa_b_qkv, swa_sinks, swa_w_o, swa_b_o, final_norm):
    b, t, d = x.shape
    depth = ffn1_norm.shape[0]
    final_g = final_norm.reshape(1, d).astype(F32)

    def ffn(xin, norm, wg, wu, wd, last):
        y = _ffn(xin.reshape(b * t, d), norm.reshape(1, d).astype(F32), wg.astype(BF16), wu.astype(BF16),
                 wd.astype(BF16), final_g, last)
        return y.reshape(b, t, d)

    for li in range(depth):
        x = ffn(x, ffn1_norm[li], ffn1_w_gate[li], ffn1_w_up[li], ffn1_w_down[li], False)
        norm = mix_norm[li].reshape(1, d).astype(F32)
        j = li // 2
        if li % 2 == 0:
            x = _hybrid(x, norm, hyb_w_in[j], gdn_conv_w[j], gdn_a_log[j], gdn_dt_bias[j], gdn_out_norm[j],
                        sc_conv_w[j], hyb_w_out[j])
        else:
            x = _swa(x, norm, swa_w_qkv[j], swa_b_qkv[j], swa_sinks[j], swa_w_o[j], swa_b_o[j])
        x = ffn(x, ffn2_norm[li], ffn2_w_gate[li], ffn2_w_up[li], ffn2_w_down[li], li == depth - 1)
    return x
```

```python
import functools

import jax
import jax.numpy as jnp
from jax import lax
from jax.experimental import pallas as pl
from jax.experimental.pallas import tpu as pltpu

F32 = jnp.float32
BF16 = jnp.bfloat16

D_MODEL = 1024
D_FF = 2816
GDN_HEADS = 4
GDN_HEAD_DIM = 128
GDN_CONV = 4
GDN_QKV = 3 * GDN_HEADS * GDN_HEAD_DIM
GDN_Z = GDN_HEADS * GDN_HEAD_DIM
SC_CHANNELS = 512
SC_CONV = 3
SWA_HEADS = 16
SWA_KV_HEADS = 4
SWA_HEAD_DIM = 64
SWA_WINDOW = 128
SWA_Q = SWA_HEADS * SWA_HEAD_DIM
SWA_KV = SWA_KV_HEADS * SWA_HEAD_DIM
ROPE_THETA = 10000.0
RMS_EPS = 1e-6
L2_EPS = 1e-6

LANES = 128
SUBLANES = 8
VMEM_LIMIT_BYTES = 56 * 1024 * 1024

FFN_ROWS = 1024
FFN_COLS = 256
HYB_ROWS = 512
GDN_UNIT = 128
SWA_ROWS = 512
SWA_PART = 256
CONV_HALO = SUBLANES

NEG_INF = float("-inf")


def _rms(x, g):
    ms = jnp.mean(x * x, axis=-1, keepdims=True)
    return x * lax.rsqrt(ms + RMS_EPS) * g


def _sigmoid(x):
    return 1.0 / (1.0 + jnp.exp(-x))


def _silu(x):
    return x * _sigmoid(x)


def _dot(a, b):
    return jnp.dot(a, b, preferred_element_type=F32)


def _dot_nt(a, b):
    return lax.dot_general(a, b, (((1,), (1,)), ((), ())), preferred_element_type=F32)


def _dot_tn(a, b):
    return lax.dot_general(a, b, (((0,), (0,)), ((), ())), preferred_element_type=F32)


def _resident(shape):
    nd = len(shape)
    return pl.BlockSpec(shape, lambda *_: (0,) * nd, pipeline_mode=pl.Buffered(1))


def _ffn_body(x_ref, g_ref, wg_ref, wu_ref, wd_ref, fg_ref, o_ref, act_ref, *, final_norm):
    half = x_ref.shape[0] // 2
    halves = (slice(0, half), slice(half, 2 * half))
    hs = [_rms(x_ref[rs, :], g_ref[...]).astype(BF16) for rs in halves]
    for c in range(D_FF // FFN_COLS):
        cols = slice(c * FFN_COLS, (c + 1) * FFN_COLS)
        for h, rs in zip(hs, halves):
            gate = _dot(h, wg_ref[:, cols])
            up = _dot(h, wu_ref[:, cols])
            act_ref[rs, cols] = (_silu(gate) * up).astype(BF16)
    for rs in halves:
        y = x_ref[rs, :] + 0.5 * _dot(act_ref[rs, :], wd_ref[...])
        if final_norm:
            y = _rms(y, fg_ref[...])
        o_ref[rs, :] = y


def _ffn(x2d, norm_g, wg, wu, wd, final_g, final_norm):
    rows = x2d.shape[0]
    assert rows % FFN_ROWS == 0
    row_spec = pl.BlockSpec((FFN_ROWS, D_MODEL), lambda i: (i, 0))
    return pl.pallas_call(
        functools.partial(_ffn_body, final_norm=final_norm),
        grid=(rows // FFN_ROWS,),
        in_specs=[row_spec, _resident((1, D_MODEL)), _resident((D_MODEL, D_FF)), _resident((D_MODEL, D_FF)),
                  _resident((D_FF, D_MODEL)), _resident((1, D_MODEL))],
        out_specs=row_spec,
        out_shape=jax.ShapeDtypeStruct((rows, D_MODEL), F32),
        scratch_shapes=[pltpu.VMEM((FFN_ROWS, D_FF), BF16)],
        compiler_params=pltpu.CompilerParams(dimension_semantics=("arbitrary",),
                                             vmem_limit_bytes=VMEM_LIMIT_BYTES),
        name="ffn_final" if final_norm else "ffn",
    )(x2d, norm_g, wg, wu, wd, final_g)


def _causal_conv(ext_ref, cur, w_ref, width, cs):
    rows = cur.shape[0]
    ext_ref[CONV_HALO:CONV_HALO + rows, cs] = cur
    acc = cur * w_ref[width - 1:width, cs]
    for j in range(width - 1):
        back = width - 1 - j
        acc = acc + ext_ref[CONV_HALO - back:CONV_HALO - back + rows, cs] * w_ref[j:j + 1, cs]
    return acc


def _conv_carry(ext_ref):
    rows = ext_ref.shape[0] - CONV_HALO
    ext_ref[0:CONV_HALO, :] = ext_ref[rows:rows + CONV_HALO, :]


def _segment_cumsum(x, seg):
    pos = lax.broadcasted_iota(jnp.int32, x.shape, 0) % seg
    shift = 1
    while shift < seg:
        x = x + jnp.where(pos >= shift, pltpu.roll(x, shift, axis=0), 0.0)
        shift *= 2
    return x


def _hyb_body(x_ref, g_ref, wqkv_ref, wzsc_ref, wba_ref, convw_ref, alog_ref, dtb_ref, onorm_ref, scw_ref,
              wout_ref, o_ref, ext_ref, ext2_ref, state_ref):
    first = pl.program_id(1) == 0

    @pl.when(first)
    def _():
        state_ref[...] = jnp.zeros(state_ref.shape, F32)
        ext_ref[0:CONV_HALO, :] = jnp.zeros((CONV_HALO, ext_ref.shape[1]), F32)
        ext2_ref[0:CONV_HALO, :] = jnp.zeros((CONV_HALO, ext2_ref.shape[1]), F32)

    rows = x_ref.shape[1]
    hd = GDN_HEAD_DIM
    unit = GDN_UNIT
    x = x_ref[0]
    h = _rms(x, g_ref[...]).astype(BF16)
    p_ba = _dot(h, wba_ref[...])
    p_qkv = _dot(h, wqkv_ref[...])
    p_zsc = _dot(h, wzsc_ref[...])

    beta_all = _sigmoid(p_ba)
    sp_in = p_ba + dtb_ref[...]
    softplus = jnp.maximum(sp_in, 0.0) + jnp.log1p(jnp.exp(-jnp.abs(sp_in)))
    g_all = -jnp.exp(alog_ref[...]) * softplus
    gc_all = _segment_cumsum(g_all, unit)
    gc_rows = gc_all.T

    cols = []
    for cg in range(GDN_QKV // hd):
        cs = slice(cg * hd, (cg + 1) * hd)
        c = _silu(_causal_conv(ext_ref, p_qkv[:, cs], convw_ref, GDN_CONV, cs))
        if cg < 2 * GDN_HEADS:
            c = c * lax.rsqrt(jnp.sum(c * c, axis=-1, keepdims=True) + L2_EPS)
        if cg < GDN_HEADS:
            c = c * (hd ** -0.5)
        cols.append(c)

    row = lax.broadcasted_iota(jnp.int32, (unit, unit), 0)
    col = lax.broadcasted_iota(jnp.int32, (unit, unit), 1)
    lower_incl = row >= col
    lower_strict = row > col
    eye = jnp.where(row == col, 1.0, 0.0)

    n_units = rows // unit
    pairs = [(u, hh) for u in range(n_units) for hh in range(GDN_HEADS)]
    pre = {}
    for u, hh in pairs:
        rs = slice(u * unit, (u + 1) * unit)
        q = cols[hh][rs]
        k = cols[GDN_HEADS + hh][rs]
        v = cols[2 * GDN_HEADS + hh][rs]
        beta = beta_all[rs, hh:hh + 1]
        gc = gc_all[rs, GDN_HEADS + hh:GDN_HEADS + hh + 1]
        gc_row = gc_rows[GDN_HEADS + hh:GDN_HEADS + hh + 1, rs]
        g_last = gc[unit - 1:unit, :]
        e_gc = jnp.exp(gc)
        kb = k * beta
        pre[u, hh] = dict(
            decay=jnp.exp(jnp.where(lower_incl, gc - gc_row, NEG_INF)),
            kbq16=jnp.concatenate([kb, q], axis=0).astype(BF16),
            k16=k.astype(BF16),
            rhs16=jnp.concatenate([v * beta, kb * e_gc], axis=-1).astype(BF16),
            qg16=(q * e_gc).astype(BF16),
            kg16=(k * jnp.exp(g_last - gc)).astype(BF16),
            s_decay=jnp.exp(g_last))

    kq = {p: _dot_nt(pre[p]["kbq16"], pre[p]["k16"]) for p in pairs}
    neg_m, attn16, inv = {}, {}, {}
    for p in pairs:
        decay = pre[p]["decay"]
        n = jnp.where(lower_strict, -(kq[p][:unit] * decay), 0.0)
        neg_m[p] = n.astype(BF16)
        attn16[p] = (kq[p][unit:] * decay).astype(BF16)
        inv[p] = eye + n
    span = 2
    while span < unit:
        resid = {p: ((eye - inv[p]) + _dot(neg_m[p], inv[p].astype(BF16))).astype(BF16) for p in pairs}
        inv = {p: inv[p] + _dot(inv[p].astype(BF16), resid[p]) for p in pairs}
        span *= 2
    sol = {p: _dot(inv[p].astype(BF16), pre[p]["rhs16"]) for p in pairs}

    gdn_out = []
    for u in range(n_units):
        rs = slice(u * unit, (u + 1) * unit)
        ups = [(u, hh) for hh in range(GDN_HEADS)]
        state = {p: state_ref[p[1]] for p in ups}
        s16 = {p: state[p].astype(BF16) for p in ups}
        wq = {p: _dot(jnp.concatenate([sol[p][:, hd:].astype(BF16), pre[p]["qg16"]], axis=0), s16[p]) for p in ups}
        v16 = {p: (sol[p][:, :hd] - wq[p][:unit]).astype(BF16) for p in ups}
        for p in ups:
            state_ref[p[1]] = state[p] * pre[p]["s_decay"] + _dot_tn(pre[p]["kg16"], v16[p])
        head_out = []
        for p in ups:
            hh = p[1]
            o = wq[p][unit:] + _dot(attn16[p], v16[p])
            o = o * lax.rsqrt(jnp.mean(o * o, axis=-1, keepdims=True) + RMS_EPS)
            head_out.append(o * onorm_ref[...] * _silu(p_zsc[rs, hh * hd:(hh + 1) * hd]))
        gdn_out.append(jnp.concatenate(head_out, axis=-1))
    gdn = jnp.concatenate(gdn_out, axis=0)

    y_parts = []
    for cg in range(SC_CHANNELS // LANES):
        cs = slice(cg * LANES, (cg + 1) * LANES)
        sc_b = p_zsc[:, GDN_Z + cg * LANES:GDN_Z + (cg + 1) * LANES]
        sc_ch = (p_zsc[:, GDN_Z + SC_CHANNELS + cg * LANES:GDN_Z + SC_CHANNELS + (cg + 1) * LANES]
                 * p_zsc[:, GDN_Z + 2 * SC_CHANNELS + cg * LANES:GDN_Z + 2 * SC_CHANNELS + (cg + 1) * LANES])
        y_parts.append(sc_b * _causal_conv(ext2_ref, sc_ch, scw_ref, SC_CONV, cs))

    mix = jnp.concatenate([gdn] + y_parts, axis=-1).astype(BF16)
    o_ref[0] = x + _dot(mix, wout_ref[...])
    _conv_carry(ext_ref)
    _conv_carry(ext2_ref)


def _hybrid(x, norm_g, w_in16, conv_w, a_log, dt_bias, out_norm, sc_conv_w, w_out16):
    b, t, _ = x.shape
    assert t % HYB_ROWS == 0 and HYB_ROWS % GDN_UNIT == 0
    n_qkvz = GDN_QKV + GDN_Z
    n_zsc = GDN_Z + 3 * SC_CHANNELS
    w_qkv = w_in16[:, :GDN_QKV]
    w_ba = jnp.pad(w_in16[:, n_qkvz:n_qkvz + 2 * GDN_HEADS], ((0, 0), (0, LANES - 2 * GDN_HEADS)))
    w_zsc = jnp.concatenate([w_in16[:, GDN_QKV:n_qkvz], w_in16[:, n_qkvz + 2 * GDN_HEADS:]], axis=1)
    lane_pad = (GDN_HEADS, LANES - 2 * GDN_HEADS)
    alog_row = jnp.pad(a_log.astype(F32), lane_pad).reshape(1, LANES)
    dtb_row = jnp.pad(dt_bias.astype(F32), lane_pad).reshape(1, LANES)
    tile = pl.BlockSpec((1, HYB_ROWS, D_MODEL), lambda bi, ti: (bi, ti, 0))
    return pl.pallas_call(
        _hyb_body,
        grid=(b, t // HYB_ROWS),
        in_specs=[tile, _resident((1, D_MODEL)), _resident((D_MODEL, GDN_QKV)), _resident((D_MODEL, n_zsc)),
                  _resident((D_MODEL, LANES)), _resident((GDN_CONV, GDN_QKV)), _resident((1, LANES)),
                  _resident((1, LANES)), _resident((1, GDN_HEAD_DIM)), _resident((SC_CONV, SC_CHANNELS)),
                  _resident((GDN_Z + SC_CHANNELS, D_MODEL))],
        out_specs=tile,
        out_shape=jax.ShapeDtypeStruct(x.shape, F32),
        scratch_shapes=[pltpu.VMEM((CONV_HALO + HYB_ROWS, GDN_QKV), F32),
                        pltpu.VMEM((CONV_HALO + HYB_ROWS, SC_CHANNELS), F32),
                        pltpu.VMEM((GDN_HEADS, GDN_HEAD_DIM, GDN_HEAD_DIM), F32)],
        compiler_params=pltpu.CompilerParams(dimension_semantics=("arbitrary", "arbitrary"),
                                             vmem_limit_bytes=VMEM_LIMIT_BYTES),
        name="hybrid_mixer",
    )(x, norm_g, w_qkv, w_zsc, w_ba, conv_w.astype(F32), alog_row, dtb_row,
      out_norm.reshape(1, GDN_HEAD_DIM).astype(F32), sc_conv_w.astype(F32), w_out16)


def _rope_body(inv_ref, cos_ref, sin_ref):
    rows = cos_ref.shape[0]
    half = SWA_HEAD_DIM // 2
    t = lax.broadcasted_iota(jnp.int32, (rows, LANES), 0) + pl.program_id(0) * rows
    lane = lax.broadcasted_iota(jnp.int32, (rows, LANES), 1)
    ang = t.astype(F32) * inv_ref[...]
    cos_ref[...] = jnp.cos(ang)
    sin_ref[...] = jnp.where(lane % SWA_HEAD_DIM < half, -1.0, 1.0) * jnp.sin(ang)


def _rope_tables(t):
    half = SWA_HEAD_DIM // 2
    inv = ROPE_THETA ** (-jnp.arange(half, dtype=F32) / half)
    inv_row = jnp.tile(inv, LANES // half).reshape(1, LANES)
    rows = min(t, 1024)
    assert t % rows == 0
    spec = pl.BlockSpec((rows, LANES), lambda i: (i, 0))
    return pl.pallas_call(
        _rope_body,
        grid=(t // rows,),
        in_specs=[pl.BlockSpec((1, LANES), lambda i: (0, 0))],
        out_specs=[spec, spec],
        out_shape=[jax.ShapeDtypeStruct((t, LANES), F32)] * 2,
        compiler_params=pltpu.CompilerParams(dimension_semantics=("arbitrary",)),
        name="rope_tables",
    )(inv_row)


def _rope_apply(x, cos, sin_signed):
    half = SWA_HEAD_DIM // 2
    width = x.shape[1]
    lane = lax.broadcasted_iota(jnp.int32, x.shape, 1)
    swapped = jnp.where(lane % SWA_HEAD_DIM < half,
                        pltpu.roll(x, width - half, axis=1),
                        pltpu.roll(x, half, axis=1))
    reps = width // LANES
    cos_w = jnp.concatenate([cos] * reps, axis=1)
    sin_w = jnp.concatenate([sin_signed] * reps, axis=1)
    return x * cos_w + swapped * sin_w


def _swa_body(sinks_ref, x_ref, g_ref, wqkv_ref, bqkv_ref, cos_ref, sin_ref, wo_ref, bo_ref, o_ref,
              kpad_ref, vpad_ref, attn_ref):
    first = pl.program_id(1) == 0
    rows = x_ref.shape[1]
    win = SWA_WINDOW
    d = SWA_HEAD_DIM
    grp = SWA_HEADS // SWA_KV_HEADS
    n_items = 2 * SWA_KV_HEADS

    @pl.when(first)
    def _():
        kpad_ref[:, 0:win, :] = jnp.zeros((n_items, win, LANES), BF16)
        vpad_ref[:, 0:win, :] = jnp.zeros((n_items, win, LANES), BF16)

    low_half = lax.broadcasted_iota(jnp.int32, (SWA_PART, LANES), 1) < d

    def project(part):
        r0 = part * SWA_PART
        h = _rms(x_ref[0, r0:r0 + SWA_PART, :], g_ref[...]).astype(BF16)
        qkv = _dot(h, wqkv_ref[...]) + bqkv_ref[...]
        cos = cos_ref[r0:r0 + SWA_PART, :]
        sin = sin_ref[r0:r0 + SWA_PART, :]
        q = (_rope_apply(qkv[:, :SWA_Q], cos, sin) * (d ** -0.5)).astype(BF16)
        k = _rope_apply(qkv[:, SWA_Q:SWA_Q + SWA_KV], cos, sin).astype(BF16)
        v = qkv[:, SWA_Q + SWA_KV:].astype(BF16)
        dst_rows = slice(win + r0, win + r0 + SWA_PART)
        for src, dst in ((k, kpad_ref), (v, vpad_ref)):
            for pair in range(SWA_KV // LANES):
                col = src[:, pair * LANES:(pair + 1) * LANES]
                swapped = jnp.concatenate([col[:, d:], col[:, :d]], axis=1)
                zero = jnp.zeros_like(col)
                dst[4 * pair + 0, dst_rows, :] = jnp.where(low_half, col, zero)
                dst[4 * pair + 1, dst_rows, :] = jnp.where(low_half, zero, swapped)
                dst[4 * pair + 2, dst_rows, :] = jnp.where(low_half, swapped, zero)
                dst[4 * pair + 3, dst_rows, :] = jnp.where(low_half, zero, col)
        return q

    items = [(kv, par) for kv in range(SWA_KV_HEADS) for par in range(2)]

    def qk_stage(q, blk):
        lr = (blk * win) % SWA_PART
        out = {}
        for kv, par in items:
            c0 = kv * grp * d
            q2 = jnp.concatenate([q[lr:lr + win, c0:c0 + LANES], q[lr:lr + win, c0 + LANES:c0 + 2 * LANES]], axis=0)
            out[kv, par] = _dot_nt(q2, kpad_ref[2 * kv + par, blk * win:(blk + 2) * win, :])
        return out

    qi = lax.broadcasted_iota(jnp.int32, (2 * win, win), 0) % win
    kj = lax.broadcasted_iota(jnp.int32, (2 * win, win), 1)
    from_prev = kj > qi
    srow = lax.broadcasted_iota(jnp.int32, (2 * win, 1), 0)
    prev_bias = jnp.where(first, NEG_INF, 0.0)

    def softmax_pv_stage(scores, blk):
        probs = {}
        for kv, par in items:
            s = scores[kv, par]
            s_prev = s[:, :win] + prev_bias if blk == 0 else s[:, :win]
            folded = jnp.where(from_prev, s_prev, s[:, win:])
            sink = jnp.where(srow < win, sinks_ref[kv * grp + par], sinks_ref[kv * grp + 2 + par])
            mx = jnp.maximum(jnp.max(folded, axis=-1, keepdims=True), sink)
            e = jnp.exp(folded - mx)
            inv_den = 1.0 / (jnp.sum(e, axis=-1, keepdims=True) + jnp.exp(sink - mx))
            p = (e * inv_den).astype(BF16)
            zero = jnp.zeros_like(p)
            probs[kv, par] = jnp.concatenate([jnp.where(from_prev, p, zero), jnp.where(from_prev, zero, p)], axis=1)
        for kv in range(SWA_KV_HEADS):
            c0 = kv * grp * d
            out = (_dot(probs[kv, 0], vpad_ref[2 * kv, blk * win:(blk + 2) * win, :])
                   + _dot(probs[kv, 1], vpad_ref[2 * kv + 1, blk * win:(blk + 2) * win, :]))
            attn_ref[blk * win:(blk + 1) * win, c0:c0 + LANES] = out[:win].astype(BF16)
            attn_ref[blk * win:(blk + 1) * win, c0 + LANES:c0 + 2 * LANES] = out[win:].astype(BF16)

    n_parts = rows // SWA_PART
    blk_per_part = SWA_PART // win
    q = project(0)
    scores = {blk: qk_stage(q, blk) for blk in range(blk_per_part)}
    for part in range(n_parts):
        nxt = part + 1
        if nxt < n_parts:
            q = project(nxt)
        for blk in range(part * blk_per_part, nxt * blk_per_part):
            softmax_pv_stage(scores.pop(blk), blk)
        if nxt < n_parts:
            for blk in range(nxt * blk_per_part, (nxt + 1) * blk_per_part):
                scores[blk] = qk_stage(q, blk)
        rs = slice(part * SWA_PART, nxt * SWA_PART)
        o_ref[0, rs, :] = x_ref[0, rs, :] + _dot(attn_ref[rs, :], wo_ref[...]) + bo_ref[...]

    kpad_ref[:, 0:win, :] = kpad_ref[:, rows:rows + win, :]
    vpad_ref[:, 0:win, :] = vpad_ref[:, rows:rows + win, :]


def _swa(x, norm_g, w_qkv16, b_qkv, sinks, w_o16, b_o):
    b, t, _ = x.shape
    assert t % SWA_ROWS == 0 and SWA_ROWS % SWA_PART == 0 and SWA_PART % SWA_WINDOW == 0
    cos_t, sin_t = _rope_tables(t)
    n_qkv = SWA_Q + 2 * SWA_KV
    tile = pl.BlockSpec((1, SWA_ROWS, D_MODEL), lambda bi, ti, *_: (bi, ti, 0))
    table = pl.BlockSpec((SWA_ROWS, LANES), lambda bi, ti, *_: (ti, 0))
    pad_shape = (2 * SWA_KV_HEADS, SWA_WINDOW + SWA_ROWS, LANES)
    return pl.pallas_call(
        _swa_body,
        grid_spec=pltpu.PrefetchScalarGridSpec(
            num_scalar_prefetch=1,
            grid=(b, t // SWA_ROWS),
            in_specs=[tile, _resident((1, D_MODEL)), _resident((D_MODEL, n_qkv)), _resident((1, n_qkv)),
                      table, table, _resident((SWA_Q, D_MODEL)), _resident((1, D_MODEL))],
            out_specs=tile,
            scratch_shapes=[pltpu.VMEM(pad_shape, BF16), pltpu.VMEM(pad_shape, BF16),
                            pltpu.VMEM((SWA_ROWS, SWA_Q), BF16)]),
        out_shape=jax.ShapeDtypeStruct(x.shape, F32),
        compiler_params=pltpu.CompilerParams(dimension_semantics=("arbitrary", "arbitrary"),
                                             vmem_limit_bytes=VMEM_LIMIT_BYTES),
        name="swa_mixer",
    )(sinks.astype(F32), x, norm_g, w_qkv16, b_qkv.reshape(1, n_qkv).astype(F32), cos_t, sin_t,
      w_o16, b_o.reshape(1, D_MODEL).astype(F32))


def kernel(x, ffn1_norm, ffn1_w_gate, ffn1_w_up, ffn1_w_down, mix_norm, ffn2_norm, ffn2_w_gate, ffn2_w_up,
           ffn2_w_down, hyb_w_in, gdn_conv_w, gdn_a_log, gdn_dt_bias, gdn_out_norm, sc_conv_w, hyb_w_out,
           swa_w_qkv, swa_b_qkv, swa_sinks, swa_w_o, swa_b_o, final_norm):
    b, t, d = x.shape
    depth = ffn1_norm.shape[0]
    final_g = final_norm.reshape(1, d).astype(F32)
    ffn1 = [w.astype(BF16) for w in (ffn1_w_gate, ffn1_w_up, ffn1_w_down)]
    ffn2 = [w.astype(BF16) for w in (ffn2_w_gate, ffn2_w_up, ffn2_w_down)]
    hyb_in16, hyb_out16 = hyb_w_in.astype(BF16), hyb_w_out.astype(BF16)
    swa_qkv16, swa_o16 = swa_w_qkv.astype(BF16), swa_w_o.astype(BF16)

    def ffn(xin, norm, ws, li, last):
        y = _ffn(xin.reshape(b * t, d), norm.reshape(1, d).astype(F32), ws[0][li], ws[1][li], ws[2][li],
                 final_g, last)
        return y.reshape(b, t, d)

    for li in range(depth):
        x = ffn(x, ffn1_norm[li], ffn1, li, False)
        norm = mix_norm[li].reshape(1, d).astype(F32)
        j = li // 2
        if li % 2 == 0:
            x = _hybrid(x, norm, hyb_in16[j], gdn_conv_w[j], gdn_a_log[j], gdn_dt_bias[j], gdn_out_norm[j],
                        sc_conv_w[j], hyb_out16[j])
        else:
            x = _swa(x, norm, swa_qkv16[j], swa_b_qkv[j], swa_sinks[j], swa_o16[j], swa_b_o[j])
        x = ffn(x, ffn2_norm[li], ffn2, li, li == depth - 1)
    return x
```

```python
import functools

import jax
import jax.numpy as jnp
from jax import lax
from jax.experimental import pallas as pl
from jax.experimental.pallas import tpu as pltpu

F32 = jnp.float32
BF16 = jnp.bfloat16

D_MODEL = 1024
D_FF = 2816
GDN_HEADS = 4
GDN_HEAD_DIM = 128
GDN_CONV = 4
GDN_QKV = 3 * GDN_HEADS * GDN_HEAD_DIM
GDN_Z = GDN_HEADS * GDN_HEAD_DIM
SC_CHANNELS = 512
SC_CONV = 3
SWA_HEADS = 16
SWA_KV_HEADS = 4
SWA_HEAD_DIM = 64
SWA_WINDOW = 128
SWA_Q = SWA_HEADS * SWA_HEAD_DIM
SWA_KV = SWA_KV_HEADS * SWA_HEAD_DIM
ROPE_THETA = 10000.0
RMS_EPS = 1e-6
L2_EPS = 1e-6

LANES = 128
SUBLANES = 8
VMEM_LIMIT_BYTES = 56 * 1024 * 1024

FFN_ROWS = 1024
FFN_COLS = 256
HYB_ROWS = 512
HYB_OUT_COLS = 256
GDN_UNIT = 128
SWA_ROWS = 1024
SWA_PART = 256
CONV_HALO = SUBLANES

NEG_INF = float("-inf")


def _rms(x, g):
    ms = jnp.mean(x * x, axis=-1, keepdims=True)
    return x * lax.rsqrt(ms + RMS_EPS) * g


def _sigmoid(x):
    return 1.0 / (1.0 + jnp.exp(-x))


def _silu(x):
    return x * _sigmoid(x)


def _dot(a, b):
    return jnp.dot(a, b, preferred_element_type=F32)


def _dot_nt(a, b):
    return lax.dot_general(a, b, (((1,), (1,)), ((), ())), preferred_element_type=F32)


def _dot_tn(a, b):
    return lax.dot_general(a, b, (((0,), (0,)), ((), ())), preferred_element_type=F32)


def _resident(shape):
    nd = len(shape)
    return pl.BlockSpec(shape, lambda *_: (0,) * nd, pipeline_mode=pl.Buffered(1))


def _resident_layer(shape, layer):
    nd = len(shape)
    return pl.BlockSpec((pl.squeezed,) + tuple(shape), lambda *_: (layer,) + (0,) * nd,
                        pipeline_mode=pl.Buffered(1))


def _ffn_body(x_ref, g_ref, wg_ref, wu_ref, wd_ref, fg_ref, o_ref, act_ref, *, final_norm):
    half = x_ref.shape[0] // 2
    halves = (slice(0, half), slice(half, 2 * half))
    hs = [_rms(x_ref[rs, :], g_ref[...]).astype(BF16) for rs in halves]
    for c in range(D_FF // FFN_COLS):
        cols = slice(c * FFN_COLS, (c + 1) * FFN_COLS)
        for h, rs in zip(hs, halves):
            gate = _dot(h, wg_ref[:, cols])
            up = _dot(h, wu_ref[:, cols])
            act_ref[rs, cols] = (_silu(gate) * up).astype(BF16)
    for rs in halves:
        y = x_ref[rs, :] + 0.5 * _dot(act_ref[rs, :], wd_ref[...])
        if final_norm:
            y = _rms(y, fg_ref[...])
        o_ref[rs, :] = y


def _ffn(x2d, norm_g, wg, wu, wd, layer, final_g, final_norm):
    rows = x2d.shape[0]
    assert rows % FFN_ROWS == 0
    row_spec = pl.BlockSpec((FFN_ROWS, D_MODEL), lambda i: (i, 0))
    return pl.pallas_call(
        functools.partial(_ffn_body, final_norm=final_norm),
        grid=(rows // FFN_ROWS,),
        in_specs=[row_spec, _resident((1, D_MODEL)), _resident_layer((D_MODEL, D_FF), layer),
                  _resident_layer((D_MODEL, D_FF), layer), _resident_layer((D_FF, D_MODEL), layer),
                  _resident((1, D_MODEL))],
        out_specs=row_spec,
        out_shape=jax.ShapeDtypeStruct((rows, D_MODEL), F32),
        scratch_shapes=[pltpu.VMEM((FFN_ROWS, D_FF), BF16)],
        compiler_params=pltpu.CompilerParams(dimension_semantics=("arbitrary",),
                                             vmem_limit_bytes=VMEM_LIMIT_BYTES),
        name="ffn_final" if final_norm else "ffn",
    )(x2d, norm_g, wg, wu, wd, final_g)


def _causal_conv(ext_ref, cur, w_ref, width, cs):
    rows = cur.shape[0]
    ext_ref[CONV_HALO:CONV_HALO + rows, cs] = cur
    acc = cur * w_ref[width - 1:width, cs]
    for j in range(width - 1):
        back = width - 1 - j
        acc = acc + ext_ref[CONV_HALO - back:CONV_HALO - back + rows, cs] * w_ref[j:j + 1, cs]
    return acc


def _conv_carry(ext_ref):
    rows = ext_ref.shape[0] - CONV_HALO
    ext_ref[0:CONV_HALO, :] = ext_ref[rows:rows + CONV_HALO, :]


def _segment_cumsum(x, seg):
    pos = lax.broadcasted_iota(jnp.int32, x.shape, 0) % seg
    shift = 1
    while shift < seg:
        x = x + jnp.where(pos >= shift, pltpu.roll(x, shift, axis=0), 0.0)
        shift *= 2
    return x


def _hyb_body(x_ref, g_ref, wqkv_ref, wzsc_ref, wba_ref, convw_ref, alog_ref, dtb_ref, onorm_ref, scw_ref,
              wout_ref, o_ref, ext_ref, ext2_ref, state_ref):
    first = pl.program_id(1) == 0

    @pl.when(first)
    def _():
        state_ref[...] = jnp.zeros(state_ref.shape, F32)
        ext_ref[0:CONV_HALO, :] = jnp.zeros((CONV_HALO, ext_ref.shape[1]), F32)
        ext2_ref[0:CONV_HALO, :] = jnp.zeros((CONV_HALO, ext2_ref.shape[1]), F32)

    rows = x_ref.shape[1]
    hd = GDN_HEAD_DIM
    unit = GDN_UNIT
    x = x_ref[0]
    h = _rms(x, g_ref[...]).astype(BF16)
    p_ba = _dot(h, wba_ref[...])
    p_qkv = _dot(h, wqkv_ref[...])
    p_zsc = _dot(h, wzsc_ref[...])

    beta_all = _sigmoid(p_ba)
    sp_in = p_ba + dtb_ref[...]
    softplus = jnp.maximum(sp_in, 0.0) + jnp.log1p(jnp.exp(-jnp.abs(sp_in)))
    g_all = -jnp.exp(alog_ref[...]) * softplus
    gc_all = _segment_cumsum(g_all, unit)
    gc_rows = gc_all.T

    cols = []
    for cg in range(GDN_QKV // hd):
        cs = slice(cg * hd, (cg + 1) * hd)
        c = _silu(_causal_conv(ext_ref, p_qkv[:, cs], convw_ref, GDN_CONV, cs))
        if cg < 2 * GDN_HEADS:
            c = c * lax.rsqrt(jnp.sum(c * c, axis=-1, keepdims=True) + L2_EPS)
        if cg < GDN_HEADS:
            c = c * (hd ** -0.5)
        cols.append(c)

    row = lax.broadcasted_iota(jnp.int32, (unit, unit), 0)
    col = lax.broadcasted_iota(jnp.int32, (unit, unit), 1)
    lower_incl = row >= col
    lower_strict = row > col
    eye = jnp.where(row == col, 1.0, 0.0)

    n_units = rows // unit
    pairs = [(u, hh) for u in range(n_units) for hh in range(GDN_HEADS)]
    pre = {}
    for u, hh in pairs:
        rs = slice(u * unit, (u + 1) * unit)
        q = cols[hh][rs]
        k = cols[GDN_HEADS + hh][rs]
        v = cols[2 * GDN_HEADS + hh][rs]
        beta = beta_all[rs, hh:hh + 1]
        gc = gc_all[rs, GDN_HEADS + hh:GDN_HEADS + hh + 1]
        gc_row = gc_rows[GDN_HEADS + hh:GDN_HEADS + hh + 1, rs]
        g_last = gc[unit - 1:unit, :]
        e_gc = jnp.exp(gc)
        kb = k * beta
        pre[u, hh] = dict(
            decay=jnp.exp(jnp.where(lower_incl, gc - gc_row, NEG_INF)),
            kbq16=jnp.concatenate([kb, q], axis=0).astype(BF16),
            k16=k.astype(BF16),
            rhs16=jnp.concatenate([v * beta, kb * e_gc], axis=-1).astype(BF16),
            qg16=(q * e_gc).astype(BF16),
            kg16=(k * jnp.exp(g_last - gc)).astype(BF16),
            s_decay=jnp.exp(g_last))

    kq = {p: _dot_nt(pre[p]["kbq16"], pre[p]["k16"]) for p in pairs}
    neg_m, attn16, inv = {}, {}, {}
    for p in pairs:
        decay = pre[p]["decay"]
        n = jnp.where(lower_strict, -(kq[p][:unit] * decay), 0.0)
        neg_m[p] = n.astype(BF16)
        attn16[p] = (kq[p][unit:] * decay).astype(BF16)
        inv[p] = eye + n
    span = 2
    while span < unit:
        resid = {p: ((eye - inv[p]) + _dot(neg_m[p], inv[p].astype(BF16))).astype(BF16) for p in pairs}
        inv = {p: inv[p] + _dot(inv[p].astype(BF16), resid[p]) for p in pairs}
        span *= 2

    y_parts = []
    for cg in range(SC_CHANNELS // LANES):
        cs = slice(cg * LANES, (cg + 1) * LANES)
        sc_b = p_zsc[:, GDN_Z + cg * LANES:GDN_Z + (cg + 1) * LANES]
        sc_ch = (p_zsc[:, GDN_Z + SC_CHANNELS + cg * LANES:GDN_Z + SC_CHANNELS + (cg + 1) * LANES]
                 * p_zsc[:, GDN_Z + 2 * SC_CHANNELS + cg * LANES:GDN_Z + 2 * SC_CHANNELS + (cg + 1) * LANES])
        y_parts.append(sc_b * _causal_conv(ext2_ref, sc_ch, scw_ref, SC_CONV, cs))
    y_sc16 = jnp.concatenate(y_parts, axis=-1).astype(BF16)

    sol, wq, v16, gdn_out = {}, {}, {}, {}

    def solve(u):
        for hh in range(GDN_HEADS):
            sol[u, hh] = _dot(inv[u, hh].astype(BF16), pre[u, hh]["rhs16"])

    def chain_read(u):
        for hh in range(GDN_HEADS):
            p = (u, hh)
            s16 = state_ref[hh].astype(BF16)
            wq[p] = _dot(jnp.concatenate([sol[p][:, hd:].astype(BF16), pre[p]["qg16"]], axis=0), s16)

    def chain_write(u):
        for hh in range(GDN_HEADS):
            p = (u, hh)
            v16[p] = (sol[p][:, :hd] - wq[p][:unit]).astype(BF16)
            state_ref[hh] = state_ref[hh] * pre[p]["s_decay"] + _dot_tn(pre[p]["kg16"], v16[p])

    def unit_output(u):
        rs = slice(u * unit, (u + 1) * unit)
        heads = []
        for hh in range(GDN_HEADS):
            p = (u, hh)
            o = wq[p][unit:] + _dot(attn16[p], v16[p])
            o = o * lax.rsqrt(jnp.mean(o * o, axis=-1, keepdims=True) + RMS_EPS)
            heads.append(o * onorm_ref[...] * _silu(p_zsc[rs, hh * hd:(hh + 1) * hd]))
        gdn_out[u] = jnp.concatenate(heads, axis=-1).astype(BF16)

    def project_out(u0, col):
        rs = slice(u0 * unit, (u0 + 2) * unit)
        mix = jnp.concatenate([jnp.concatenate([gdn_out[u0], gdn_out[u0 + 1]], axis=0), y_sc16[rs]], axis=-1)
        cs = slice(col, col + HYB_OUT_COLS)
        o_ref[0, rs, cs] = x_ref[0, rs, cs] + _dot(mix, wout_ref[:, cs])

    assert n_units % 2 == 0
    fillers = [functools.partial(solve, u) for u in range(1, n_units)]
    solve(0)
    for u in range(n_units):
        if (u, 0) not in sol:
            fillers = [f for f in fillers if not (f.func is solve and f.args == (u,))]
            solve(u)
        for step in (chain_read, chain_write):
            step(u)
            if fillers:
                fillers.pop(0)()
        fillers.append(functools.partial(unit_output, u))
        if u % 2 == 1:
            fillers += [functools.partial(project_out, u - 1, col) for col in range(0, D_MODEL, HYB_OUT_COLS)]
    for f in fillers:
        f()
    _conv_carry(ext_ref)
    _conv_carry(ext2_ref)


def _hybrid(x, norm_g, w_in16, conv_w, a_log, dt_bias, out_norm, sc_conv_w, w_out16):
    b, t, _ = x.shape
    assert t % HYB_ROWS == 0 and HYB_ROWS % GDN_UNIT == 0
    n_qkvz = GDN_QKV + GDN_Z
    n_zsc = GDN_Z + 3 * SC_CHANNELS
    w_qkv = w_in16[:, :GDN_QKV]
    w_ba = jnp.pad(w_in16[:, n_qkvz:n_qkvz + 2 * GDN_HEADS], ((0, 0), (0, LANES - 2 * GDN_HEADS)))
    w_zsc = jnp.concatenate([w_in16[:, GDN_QKV:n_qkvz], w_in16[:, n_qkvz + 2 * GDN_HEADS:]], axis=1)
    lane_pad = (GDN_HEADS, LANES - 2 * GDN_HEADS)
    alog_row = jnp.pad(a_log.astype(F32), lane_pad).reshape(1, LANES)
    dtb_row = jnp.pad(dt_bias.astype(F32), lane_pad).reshape(1, LANES)
    tile = pl.BlockSpec((1, HYB_ROWS, D_MODEL), lambda bi, ti: (bi, ti, 0))
    return pl.pallas_call(
        _hyb_body,
        grid=(b, t // HYB_ROWS),
        in_specs=[tile, _resident((1, D_MODEL)), _resident((D_MODEL, GDN_QKV)), _resident((D_MODEL, n_zsc)),
                  _resident((D_MODEL, LANES)), _resident((GDN_CONV, GDN_QKV)), _resident((1, LANES)),
                  _resident((1, LANES)), _resident((1, GDN_HEAD_DIM)), _resident((SC_CONV, SC_CHANNELS)),
                  _resident((GDN_Z + SC_CHANNELS, D_MODEL))],
        out_specs=tile,
        out_shape=jax.ShapeDtypeStruct(x.shape, F32),
        scratch_shapes=[pltpu.VMEM((CONV_HALO + HYB_ROWS, GDN_QKV), F32),
                        pltpu.VMEM((CONV_HALO + HYB_ROWS, SC_CHANNELS), F32),
                        pltpu.VMEM((GDN_HEADS, GDN_HEAD_DIM, GDN_HEAD_DIM), F32)],
        compiler_params=pltpu.CompilerParams(dimension_semantics=("arbitrary", "arbitrary"),
                                             vmem_limit_bytes=VMEM_LIMIT_BYTES),
        name="hybrid_mixer",
    )(x, norm_g, w_qkv, w_zsc, w_ba, conv_w.astype(F32), alog_row, dtb_row,
      out_norm.reshape(1, GDN_HEAD_DIM).astype(F32), sc_conv_w.astype(F32), w_out16)


def _rope_body(inv_ref, cos_ref, sin_ref):
    rows = cos_ref.shape[0]
    half = SWA_HEAD_DIM // 2
    t = lax.broadcasted_iota(jnp.int32, (rows, LANES), 0) + pl.program_id(0) * rows
    lane = lax.broadcasted_iota(jnp.int32, (rows, LANES), 1)
    ang = t.astype(F32) * inv_ref[...]
    cos_ref[...] = jnp.cos(ang)
    sin_ref[...] = jnp.where(lane % SWA_HEAD_DIM < half, -1.0, 1.0) * jnp.sin(ang)


def _rope_tables(t):
    half = SWA_HEAD_DIM // 2
    inv = ROPE_THETA ** (-jnp.arange(half, dtype=F32) / half)
    inv_row = jnp.tile(inv, LANES // half).reshape(1, LANES)
    rows = min(t, 1024)
    assert t % rows == 0
    spec = pl.BlockSpec((rows, LANES), lambda i: (i, 0))
    return pl.pallas_call(
        _rope_body,
        grid=(t // rows,),
        in_specs=[pl.BlockSpec((1, LANES), lambda i: (0, 0))],
        out_specs=[spec, spec],
        out_shape=[jax.ShapeDtypeStruct((t, LANES), F32)] * 2,
        compiler_params=pltpu.CompilerParams(dimension_semantics=("arbitrary",)),
        name="rope_tables",
    )(inv_row)


def _rope_apply(x, cos, sin_signed):
    half = SWA_HEAD_DIM // 2
    width = x.shape[1]
    lane = lax.broadcasted_iota(jnp.int32, x.shape, 1)
    swapped = jnp.where(lane % SWA_HEAD_DIM < half,
                        pltpu.roll(x, width - half, axis=1),
                        pltpu.roll(x, half, axis=1))
    reps = width // LANES
    cos_w = jnp.concatenate([cos] * reps, axis=1)
    sin_w = jnp.concatenate([sin_signed] * reps, axis=1)
    return x * cos_w + swapped * sin_w


def _swa_body(sinks_ref, x_ref, g_ref, wqkv_ref, bqkv_ref, cos_ref, sin_ref, wo_ref, bo_ref, o_ref,
              kpad_ref, vpad_ref, attn_ref):
    first = pl.program_id(1) == 0
    rows = x_ref.shape[1]
    win = SWA_WINDOW
    d = SWA_HEAD_DIM
    grp = SWA_HEADS // SWA_KV_HEADS
    n_items = 2 * SWA_KV_HEADS

    @pl.when(first)
    def _():
        kpad_ref[:, 0:win, :] = jnp.zeros((n_items, win, LANES), BF16)
        vpad_ref[:, 0:win, :] = jnp.zeros((n_items, win, LANES), BF16)

    low_half = lax.broadcasted_iota(jnp.int32, (SWA_PART, LANES), 1) < d

    def project(part):
        r0 = part * SWA_PART
        h = _rms(x_ref[0, r0:r0 + SWA_PART, :], g_ref[...]).astype(BF16)
        qkv = _dot(h, wqkv_ref[...]) + bqkv_ref[...]
        cos = cos_ref[r0:r0 + SWA_PART, :]
        sin = sin_ref[r0:r0 + SWA_PART, :]
        q = (_rope_apply(qkv[:, :SWA_Q], cos, sin) * (d ** -0.5)).astype(BF16)
        k = _rope_apply(qkv[:, SWA_Q:SWA_Q + SWA_KV], cos, sin).astype(BF16)
        v = qkv[:, SWA_Q + SWA_KV:].astype(BF16)
        dst_rows = slice(win + r0, win + r0 + SWA_PART)
        for src, dst in ((k, kpad_ref), (v, vpad_ref)):
            for pair in range(SWA_KV // LANES):
                col = src[:, pair * LANES:(pair + 1) * LANES]
                swapped = jnp.concatenate([col[:, d:], col[:, :d]], axis=1)
                zero = jnp.zeros_like(col)
                dst[4 * pair + 0, dst_rows, :] = jnp.where(low_half, col, zero)
                dst[4 * pair + 1, dst_rows, :] = jnp.where(low_half, zero, swapped)
                dst[4 * pair + 2, dst_rows, :] = jnp.where(low_half, swapped, zero)
                dst[4 * pair + 3, dst_rows, :] = jnp.where(low_half, zero, col)
        return q

    items = [(kv, par) for kv in range(SWA_KV_HEADS) for par in range(2)]

    def qk_stage(q, blk):
        lr = (blk * win) % SWA_PART
        out = {}
        for kv, par in items:
            c0 = kv * grp * d
            q2 = jnp.concatenate([q[lr:lr + win, c0:c0 + LANES], q[lr:lr + win, c0 + LANES:c0 + 2 * LANES]], axis=0)
            out[kv, par] = _dot_nt(q2, kpad_ref[2 * kv + par, blk * win:(blk + 2) * win, :])
        return out

    qi = lax.broadcasted_iota(jnp.int32, (2 * win, win), 0) % win
    kj = lax.broadcasted_iota(jnp.int32, (2 * win, win), 1)
    from_prev = kj > qi
    srow = lax.broadcasted_iota(jnp.int32, (2 * win, 1), 0)
    prev_bias = jnp.where(first, NEG_INF, 0.0)

    def softmax_pv_stage(scores, blk):
        probs = {}
        for kv, par in items:
            s = scores[kv, par]
            s_prev = s[:, :win] + prev_bias if blk == 0 else s[:, :win]
            folded = jnp.where(from_prev, s_prev, s[:, win:])
            sink = jnp.where(srow < win, sinks_ref[kv * grp + par], sinks_ref[kv * grp + 2 + par])
            mx = jnp.maximum(jnp.max(folded, axis=-1, keepdims=True), sink)
            e = jnp.exp(folded - mx)
            inv_den = 1.0 / (jnp.sum(e, axis=-1, keepdims=True) + jnp.exp(sink - mx))
            p = (e * inv_den).astype(BF16)
            zero = jnp.zeros_like(p)
            probs[kv, par] = jnp.concatenate([jnp.where(from_prev, p, zero), jnp.where(from_prev, zero, p)], axis=1)
        for kv in range(SWA_KV_HEADS):
            c0 = kv * grp * d
            out = (_dot(probs[kv, 0], vpad_ref[2 * kv, blk * win:(blk + 2) * win, :])
                   + _dot(probs[kv, 1], vpad_ref[2 * kv + 1, blk * win:(blk + 2) * win, :]))
            attn_ref[blk * win:(blk + 1) * win, c0:c0 + LANES] = out[:win].astype(BF16)
            attn_ref[blk * win:(blk + 1) * win, c0 + LANES:c0 + 2 * LANES] = out[win:].astype(BF16)

    n_parts = rows // SWA_PART
    blk_per_part = SWA_PART // win
    q = project(0)
    scores = {blk: qk_stage(q, blk) for blk in range(blk_per_part)}
    for part in range(n_parts):
        nxt = part + 1
        if nxt < n_parts:
            q = project(nxt)
        for blk in range(part * blk_per_part, nxt * blk_per_part):
            softmax_pv_stage(scores.pop(blk), blk)
        if nxt < n_parts:
            for blk in range(nxt * blk_per_part, (nxt + 1) * blk_per_part):
                scores[blk] = qk_stage(q, blk)
        rs = slice(part * SWA_PART, nxt * SWA_PART)
        o_ref[0, rs, :] = x_ref[0, rs, :] + _dot(attn_ref[rs, :], wo_ref[...]) + bo_ref[...]

    kpad_ref[:, 0:win, :] = kpad_ref[:, rows:rows + win, :]
    vpad_ref[:, 0:win, :] = vpad_ref[:, rows:rows + win, :]


def _swa(x, norm_g, w_qkv16, b_qkv, sinks, w_o16, b_o):
    b, t, _ = x.shape
    assert t % SWA_ROWS == 0 and SWA_ROWS % SWA_PART == 0 and SWA_PART % SWA_WINDOW == 0
    cos_t, sin_t = _rope_tables(t)
    n_qkv = SWA_Q + 2 * SWA_KV
    tile = pl.BlockSpec((1, SWA_ROWS, D_MODEL), lambda bi, ti, *_: (bi, ti, 0))
    table = pl.BlockSpec((SWA_ROWS, LANES), lambda bi, ti, *_: (ti, 0))
    pad_shape = (2 * SWA_KV_HEADS, SWA_WINDOW + SWA_ROWS, LANES)
    return pl.pallas_call(
        _swa_body,
        grid_spec=pltpu.PrefetchScalarGridSpec(
            num_scalar_prefetch=1,
            grid=(b, t // SWA_ROWS),
            in_specs=[tile, _resident((1, D_MODEL)), _resident((D_MODEL, n_qkv)), _resident((1, n_qkv)),
                      table, table, _resident((SWA_Q, D_MODEL)), _resident((1, D_MODEL))],
            out_specs=tile,
            scratch_shapes=[pltpu.VMEM(pad_shape, BF16), pltpu.VMEM(pad_shape, BF16),
                            pltpu.VMEM((SWA_ROWS, SWA_Q), BF16)]),
        out_shape=jax.ShapeDtypeStruct(x.shape, F32),
        compiler_params=pltpu.CompilerParams(dimension_semantics=("arbitrary", "arbitrary"),
                                             vmem_limit_bytes=VMEM_LIMIT_BYTES),
        name="swa_mixer",
    )(sinks.astype(F32), x, norm_g, w_qkv16, b_qkv.reshape(1, n_qkv).astype(F32), cos_t, sin_t,
      w_o16, b_o.reshape(1, D_MODEL).astype(F32))


def kernel(x, ffn1_norm, ffn1_w_gate, ffn1_w_up, ffn1_w_down, mix_norm, ffn2_norm, ffn2_w_gate, ffn2_w_up,
           ffn2_w_down, hyb_w_in, gdn_conv_w, gdn_a_log, gdn_dt_bias, gdn_out_norm, sc_conv_w, hyb_w_out,
           swa_w_qkv, swa_b_qkv, swa_sinks, swa_w_o, swa_b_o, final_norm):
    b, t, d = x.shape
    depth = ffn1_norm.shape[0]
    final_g = final_norm.reshape(1, d).astype(F32)
    ffn1 = [w.astype(BF16) for w in (ffn1_w_gate, ffn1_w_up, ffn1_w_down)]
    ffn2 = [w.astype(BF16) for w in (ffn2_w_gate, ffn2_w_up, ffn2_w_down)]
    hyb_in16, hyb_out16 = hyb_w_in.astype(BF16), hyb_w_out.astype(BF16)
    swa_qkv16, swa_o16 = swa_w_qkv.astype(BF16), swa_w_o.astype(BF16)

    def ffn(xin, norm, ws, li, last):
        y = _ffn(xin.reshape(b * t, d), norm.reshape(1, d).astype(F32), ws[0], ws[1], ws[2], li, final_g, last)
        return y.reshape(b, t, d)

    for li in range(depth):
        x = ffn(x, ffn1_norm[li], ffn1, li, False)
        norm = mix_norm[li].reshape(1, d).astype(F32)
        j = li // 2
        if li % 2 == 0:
            x = _hybrid(x, norm, hyb_in16[j], gdn_conv_w[j], gdn_a_log[j], gdn_dt_bias[j], gdn_out_norm[j],
                        sc_conv_w[j], hyb_out16[j])
        else:
            x = _swa(x, norm, swa_qkv16[j], swa_b_qkv[j], swa_sinks[j], swa_o16[j], swa_b_o[j])
        x = ffn(x, ffn2_norm[li], ffn2, li, li == depth - 1)
    return x
```

```python
import functools

import jax
import jax.numpy as jnp
from jax import lax
from jax.experimental import pallas as pl
from jax.experimental.pallas import tpu as pltpu

F32 = jnp.float32
BF16 = jnp.bfloat16

D_MODEL = 1024
D_FF = 2816
GDN_HEADS = 4
GDN_HEAD_DIM = 128
GDN_CONV = 4
GDN_QKV = 3 * GDN_HEADS * GDN_HEAD_DIM
GDN_Z = GDN_HEADS * GDN_HEAD_DIM
SC_CHANNELS = 512
SC_CONV = 3
SWA_HEADS = 16
SWA_KV_HEADS = 4
SWA_HEAD_DIM = 64
SWA_WINDOW = 128
SWA_Q = SWA_HEADS * SWA_HEAD_DIM
SWA_KV = SWA_KV_HEADS * SWA_HEAD_DIM
ROPE_THETA = 10000.0
RMS_EPS = 1e-6
L2_EPS = 1e-6

LANES = 128
SUBLANES = 8
VMEM_LIMIT_BYTES = 56 * 1024 * 1024

FFN_ROWS = 1024
FFN_PARTS = 4
FFN_COLS = 256
HYB_ROWS = 512
HYB_OUT_COLS = 256
HYB_ZSC_COLS = 256
HYB_ZSC_EARLY = 4
HYB_ZSC_MID = 2
GDN_UNIT = 128
SWA_ROWS = 1024
SWA_PART = 256
CONV_HALO = SUBLANES

NEG_INF = float("-inf")


def _rms(x, g):
    ms = jnp.mean(x * x, axis=-1, keepdims=True)
    return x * lax.rsqrt(ms + RMS_EPS) * g


def _sigmoid(x):
    return 1.0 / (1.0 + jnp.exp(-x))


def _silu(x):
    return x * _sigmoid(x)


def _dot(a, b):
    return jnp.dot(a, b, preferred_element_type=F32)


def _dot_nt(a, b):
    return lax.dot_general(a, b, (((1,), (1,)), ((), ())), preferred_element_type=F32)


def _dot_tn(a, b):
    return lax.dot_general(a, b, (((0,), (0,)), ((), ())), preferred_element_type=F32)


def _resident(shape):
    nd = len(shape)
    return pl.BlockSpec(shape, lambda *_: (0,) * nd, pipeline_mode=pl.Buffered(1))


def _resident_layer(shape, layer):
    nd = len(shape)
    return pl.BlockSpec((pl.squeezed,) + tuple(shape), lambda *_: (layer,) + (0,) * nd,
                        pipeline_mode=pl.Buffered(1))


def _ffn_body(x_ref, g_ref, wg_ref, wu_ref, wd_ref, fg_ref, o_ref, act_ref, *, final_norm):
    part = x_ref.shape[0] // FFN_PARTS
    halves = [slice(i * part, (i + 1) * part) for i in range(FFN_PARTS)]
    hs = [_rms(x_ref[rs, :], g_ref[...]).astype(BF16) for rs in halves]
    for c in range(D_FF // FFN_COLS):
        cols = slice(c * FFN_COLS, (c + 1) * FFN_COLS)
        for h, rs in zip(hs, halves):
            gate = _dot(h, wg_ref[:, cols])
            up = _dot(h, wu_ref[:, cols])
            act_ref[rs, cols] = (_silu(gate) * up).astype(BF16)
    for rs in halves:
        y = x_ref[rs, :] + 0.5 * _dot(act_ref[rs, :], wd_ref[...])
        if final_norm:
            y = _rms(y, fg_ref[...])
        o_ref[rs, :] = y


def _ffn(x2d, norm_g, wg, wu, wd, layer, final_g, final_norm):
    rows = x2d.shape[0]
    assert rows % FFN_ROWS == 0
    row_spec = pl.BlockSpec((FFN_ROWS, D_MODEL), lambda i: (i, 0))
    return pl.pallas_call(
        functools.partial(_ffn_body, final_norm=final_norm),
        grid=(rows // FFN_ROWS,),
        in_specs=[row_spec, _resident((1, D_MODEL)), _resident_layer((D_MODEL, D_FF), layer),
                  _resident_layer((D_MODEL, D_FF), layer), _resident_layer((D_FF, D_MODEL), layer),
                  _resident((1, D_MODEL))],
        out_specs=row_spec,
        out_shape=jax.ShapeDtypeStruct((rows, D_MODEL), F32),
        scratch_shapes=[pltpu.VMEM((FFN_ROWS, D_FF), BF16)],
        compiler_params=pltpu.CompilerParams(dimension_semantics=("arbitrary",),
                                             vmem_limit_bytes=VMEM_LIMIT_BYTES),
        name="ffn_final" if final_norm else "ffn",
    )(x2d, norm_g, wg, wu, wd, final_g)


def _causal_conv(ext_ref, cur, w_ref, width, cs):
    rows = cur.shape[0]
    ext_ref[CONV_HALO:CONV_HALO + rows, cs] = cur
    acc = cur * w_ref[width - 1:width, cs]
    for j in range(width - 1):
        back = width - 1 - j
        acc = acc + ext_ref[CONV_HALO - back:CONV_HALO - back + rows, cs] * w_ref[j:j + 1, cs]
    return acc


def _conv_carry(ext_ref):
    rows = ext_ref.shape[0] - CONV_HALO
    ext_ref[0:CONV_HALO, :] = ext_ref[rows:rows + CONV_HALO, :]


def _segment_cumsum(x, seg):
    pos = lax.broadcasted_iota(jnp.int32, x.shape, 0) % seg
    shift = 1
    while shift < seg:
        x = x + jnp.where(pos >= shift, pltpu.roll(x, shift, axis=0), 0.0)
        shift *= 2
    return x


def _hyb_body(x_ref, g_ref, wqkv_ref, wzsc_ref, wba_ref, convw_ref, alog_ref, dtb_ref, onorm_ref, scw_ref,
              wout_ref, o_ref, ext_ref, ext2_ref, state_ref):
    first = pl.program_id(1) == 0

    @pl.when(first)
    def _():
        state_ref[...] = jnp.zeros(state_ref.shape, F32)
        ext_ref[0:CONV_HALO, :] = jnp.zeros((CONV_HALO, ext_ref.shape[1]), F32)
        ext2_ref[0:CONV_HALO, :] = jnp.zeros((CONV_HALO, ext2_ref.shape[1]), F32)

    rows = x_ref.shape[1]
    hd = GDN_HEAD_DIM
    unit = GDN_UNIT
    x = x_ref[0]
    h = _rms(x, g_ref[...]).astype(BF16)
    p_ba = _dot(h, wba_ref[...])
    p_qkv = _dot(h, wqkv_ref[...])
    zsc = {}

    def project_zsc(piece):
        cs = slice(piece * HYB_ZSC_COLS, (piece + 1) * HYB_ZSC_COLS)
        zsc[piece] = _dot(h, wzsc_ref[:, cs])

    for piece in range(HYB_ZSC_EARLY):
        project_zsc(piece)

    def zsc_cols(rs, start, width):
        piece, off = divmod(start, HYB_ZSC_COLS)
        assert off + width <= HYB_ZSC_COLS
        return zsc[piece][rs, off:off + width]

    beta_all = _sigmoid(p_ba)
    sp_in = p_ba + dtb_ref[...]
    softplus = jnp.maximum(sp_in, 0.0) + jnp.log1p(jnp.exp(-jnp.abs(sp_in)))
    g_all = -jnp.exp(alog_ref[...]) * softplus
    gc_all = _segment_cumsum(g_all, unit)
    gc_rows = gc_all.T

    cols = []
    for cg in range(GDN_QKV // hd):
        cs = slice(cg * hd, (cg + 1) * hd)
        c = _silu(_causal_conv(ext_ref, p_qkv[:, cs], convw_ref, GDN_CONV, cs))
        if cg < 2 * GDN_HEADS:
            c = c * lax.rsqrt(jnp.sum(c * c, axis=-1, keepdims=True) + L2_EPS)
        if cg < GDN_HEADS:
            c = c * (hd ** -0.5)
        cols.append(c)

    row = lax.broadcasted_iota(jnp.int32, (unit, unit), 0)
    col = lax.broadcasted_iota(jnp.int32, (unit, unit), 1)
    lower_incl = row >= col
    lower_strict = row > col
    eye = jnp.where(row == col, 1.0, 0.0)

    n_units = rows // unit
    pairs = [(u, hh) for u in range(n_units) for hh in range(GDN_HEADS)]
    pre = {}
    for u, hh in pairs:
        rs = slice(u * unit, (u + 1) * unit)
        q = cols[hh][rs]
        k = cols[GDN_HEADS + hh][rs]
        v = cols[2 * GDN_HEADS + hh][rs]
        beta = beta_all[rs, hh:hh + 1]
        gc = gc_all[rs, GDN_HEADS + hh:GDN_HEADS + hh + 1]
        gc_row = gc_rows[GDN_HEADS + hh:GDN_HEADS + hh + 1, rs]
        g_last = gc[unit - 1:unit, :]
        e_gc = jnp.exp(gc)
        kb = k * beta
        pre[u, hh] = dict(
            decay=jnp.exp(jnp.where(lower_incl, gc - gc_row, NEG_INF)),
            kbq16=jnp.concatenate([kb, q], axis=0).astype(BF16),
            k16=k.astype(BF16),
            rhs16=jnp.concatenate([v * beta, kb * e_gc], axis=-1).astype(BF16),
            qg16=(q * e_gc).astype(BF16),
            kg16=(k * jnp.exp(g_last - gc)).astype(BF16),
            s_decay=jnp.exp(g_last))

    kq = {p: _dot_nt(pre[p]["kbq16"], pre[p]["k16"]) for p in pairs}
    neg_m, attn16, inv = {}, {}, {}
    for p in pairs:
        decay = pre[p]["decay"]
        n = jnp.where(lower_strict, -(kq[p][:unit] * decay), 0.0)
        neg_m[p] = n.astype(BF16)
        attn16[p] = (kq[p][unit:] * decay).astype(BF16)
        inv[p] = eye + n
    for piece in range(HYB_ZSC_EARLY, HYB_ZSC_EARLY + HYB_ZSC_MID):
        project_zsc(piece)
    span = 2
    while span < unit:
        resid = {p: ((eye - inv[p]) + _dot(neg_m[p], inv[p].astype(BF16))).astype(BF16) for p in pairs}
        inv = {p: inv[p] + _dot(inv[p].astype(BF16), resid[p]) for p in pairs}
        span *= 2

    sol, wq, v16, gdn_out, y_sc16 = {}, {}, {}, {}, {}
    all_rows = slice(0, rows)

    def short_conv():
        y_parts = []
        for cg in range(SC_CHANNELS // LANES):
            cs = slice(cg * LANES, (cg + 1) * LANES)
            sc_b = zsc_cols(all_rows, GDN_Z + cg * LANES, LANES)
            sc_ch = (zsc_cols(all_rows, GDN_Z + SC_CHANNELS + cg * LANES, LANES)
                     * zsc_cols(all_rows, GDN_Z + 2 * SC_CHANNELS + cg * LANES, LANES))
            y_parts.append(sc_b * _causal_conv(ext2_ref, sc_ch, scw_ref, SC_CONV, cs))
        y_sc16[0] = jnp.concatenate(y_parts, axis=-1).astype(BF16)

    def solve(u):
        for hh in range(GDN_HEADS):
            sol[u, hh] = _dot(inv[u, hh].astype(BF16), pre[u, hh]["rhs16"])

    def chain_read(u):
        for hh in range(GDN_HEADS):
            p = (u, hh)
            s16 = state_ref[hh].astype(BF16)
            wq[p] = _dot(jnp.concatenate([sol[p][:, hd:].astype(BF16), pre[p]["qg16"]], axis=0), s16)

    def chain_write(u):
        for hh in range(GDN_HEADS):
            p = (u, hh)
            v16[p] = (sol[p][:, :hd] - wq[p][:unit]).astype(BF16)
            state_ref[hh] = state_ref[hh] * pre[p]["s_decay"] + _dot_tn(pre[p]["kg16"], v16[p])

    def unit_output(u):
        rs = slice(u * unit, (u + 1) * unit)
        heads = []
        for hh in range(GDN_HEADS):
            p = (u, hh)
            o = wq[p][unit:] + _dot(attn16[p], v16[p])
            o = o * lax.rsqrt(jnp.mean(o * o, axis=-1, keepdims=True) + RMS_EPS)
            heads.append(o * onorm_ref[...] * _silu(zsc_cols(rs, hh * hd, hd)))
        gdn_out[u] = jnp.concatenate(heads, axis=-1).astype(BF16)

    def project_out(u0, col):
        rs = slice(u0 * unit, (u0 + 2) * unit)
        mix = jnp.concatenate([jnp.concatenate([gdn_out[u0], gdn_out[u0 + 1]], axis=0), y_sc16[0][rs]], axis=-1)
        cs = slice(col, col + HYB_OUT_COLS)
        o_ref[0, rs, cs] = x_ref[0, rs, cs] + _dot(mix, wout_ref[:, cs])

    assert n_units % 2 == 0
    late_pieces = [functools.partial(project_zsc, piece)
                   for piece in range(HYB_ZSC_EARLY + HYB_ZSC_MID, (GDN_Z + 3 * SC_CHANNELS) // HYB_ZSC_COLS)]
    fillers = []
    for u in range(1, n_units):
        fillers.append(functools.partial(solve, u))
        if late_pieces:
            fillers.append(late_pieces.pop(0))
    fillers += late_pieces + [short_conv]
    solve(0)
    for u in range(n_units):
        if (u, 0) not in sol:
            fillers = [f for f in fillers if not (f.func is solve and f.args == (u,))]
            solve(u)
        for step in (chain_read, chain_write):
            step(u)
            if fillers:
                fillers.pop(0)()
        fillers.append(functools.partial(unit_output, u))
        if u % 2 == 1:
            fillers += [functools.partial(project_out, u - 1, col) for col in range(0, D_MODEL, HYB_OUT_COLS)]
    for f in fillers:
        f()
    _conv_carry(ext_ref)
    _conv_carry(ext2_ref)


def _hybrid(x, norm_g, w_in16, conv_w, a_log, dt_bias, out_norm, sc_conv_w, w_out16):
    b, t, _ = x.shape
    assert t % HYB_ROWS == 0 and HYB_ROWS % GDN_UNIT == 0
    n_qkvz = GDN_QKV + GDN_Z
    n_zsc = GDN_Z + 3 * SC_CHANNELS
    w_qkv = w_in16[:, :GDN_QKV]
    w_ba = jnp.pad(w_in16[:, n_qkvz:n_qkvz + 2 * GDN_HEADS], ((0, 0), (0, LANES - 2 * GDN_HEADS)))
    w_zsc = jnp.concatenate([w_in16[:, GDN_QKV:n_qkvz], w_in16[:, n_qkvz + 2 * GDN_HEADS:]], axis=1)
    lane_pad = (GDN_HEADS, LANES - 2 * GDN_HEADS)
    alog_row = jnp.pad(a_log.astype(F32), lane_pad).reshape(1, LANES)
    dtb_row = jnp.pad(dt_bias.astype(F32), lane_pad).reshape(1, LANES)
    tile = pl.BlockSpec((1, HYB_ROWS, D_MODEL), lambda bi, ti: (bi, ti, 0))
    return pl.pallas_call(
        _hyb_body,
        grid=(b, t // HYB_ROWS),
        in_specs=[tile, _resident((1, D_MODEL)), _resident((D_MODEL, GDN_QKV)), _resident((D_MODEL, n_zsc)),
                  _resident((D_MODEL, LANES)), _resident((GDN_CONV, GDN_QKV)), _resident((1, LANES)),
                  _resident((1, LANES)), _resident((1, GDN_HEAD_DIM)), _resident((SC_CONV, SC_CHANNELS)),
                  _resident((GDN_Z + SC_CHANNELS, D_MODEL))],
        out_specs=tile,
        out_shape=jax.ShapeDtypeStruct(x.shape, F32),
        scratch_shapes=[pltpu.VMEM((CONV_HALO + HYB_ROWS, GDN_QKV), F32),
                        pltpu.VMEM((CONV_HALO + HYB_ROWS, SC_CHANNELS), F32),
                        pltpu.VMEM((GDN_HEADS, GDN_HEAD_DIM, GDN_HEAD_DIM), F32)],
        compiler_params=pltpu.CompilerParams(dimension_semantics=("arbitrary", "arbitrary"),
                                             vmem_limit_bytes=VMEM_LIMIT_BYTES),
        name="hybrid_mixer",
    )(x, norm_g, w_qkv, w_zsc, w_ba, conv_w.astype(F32), alog_row, dtb_row,
      out_norm.reshape(1, GDN_HEAD_DIM).astype(F32), sc_conv_w.astype(F32), w_out16)


def _rope_body(inv_ref, cos_ref, sin_ref):
    rows = cos_ref.shape[0]
    half = SWA_HEAD_DIM // 2
    t = lax.broadcasted_iota(jnp.int32, (rows, LANES), 0) + pl.program_id(0) * rows
    lane = lax.broadcasted_iota(jnp.int32, (rows, LANES), 1)
    ang = t.astype(F32) * inv_ref[...]
    cos_ref[...] = jnp.cos(ang)
    sin_ref[...] = jnp.where(lane % SWA_HEAD_DIM < half, -1.0, 1.0) * jnp.sin(ang)


def _rope_tables(t):
    half = SWA_HEAD_DIM // 2
    inv = ROPE_THETA ** (-jnp.arange(half, dtype=F32) / half)
    inv_row = jnp.tile(inv, LANES // half).reshape(1, LANES)
    rows = min(t, 1024)
    assert t % rows == 0
    spec = pl.BlockSpec((rows, LANES), lambda i: (i, 0))
    return pl.pallas_call(
        _rope_body,
        grid=(t // rows,),
        in_specs=[pl.BlockSpec((1, LANES), lambda i: (0, 0))],
        out_specs=[spec, spec],
        out_shape=[jax.ShapeDtypeStruct((t, LANES), F32)] * 2,
        compiler_params=pltpu.CompilerParams(dimension_semantics=("arbitrary",)),
        name="rope_tables",
    )(inv_row)


def _rope_apply(x, cos, sin_signed):
    half = SWA_HEAD_DIM // 2
    width = x.shape[1]
    lane = lax.broadcasted_iota(jnp.int32, x.shape, 1)
    swapped = jnp.where(lane % SWA_HEAD_DIM < half,
                        pltpu.roll(x, width - half, axis=1),
                        pltpu.roll(x, half, axis=1))
    reps = width // LANES
    cos_w = jnp.concatenate([cos] * reps, axis=1)
    sin_w = jnp.concatenate([sin_signed] * reps, axis=1)
    return x * cos_w + swapped * sin_w


def _swa_body(sinks_ref, x_ref, g_ref, wqkv_ref, bqkv_ref, cos_ref, sin_ref, wo_ref, bo_ref, o_ref,
              kpad_ref, vpad_ref, attn_ref):
    first = pl.program_id(1) == 0
    rows = x_ref.shape[1]
    win = SWA_WINDOW
    d = SWA_HEAD_DIM
    grp = SWA_HEADS // SWA_KV_HEADS
    n_items = 2 * SWA_KV_HEADS

    @pl.when(first)
    def _():
        kpad_ref[:, 0:win, :] = jnp.zeros((n_items, win, LANES), BF16)
        vpad_ref[:, 0:win, :] = jnp.zeros((n_items, win, LANES), BF16)

    low_half = lax.broadcasted_iota(jnp.int32, (SWA_PART, LANES), 1) < d

    def project(part):
        r0 = part * SWA_PART
        h = _rms(x_ref[0, r0:r0 + SWA_PART, :], g_ref[...]).astype(BF16)
        qkv = _dot(h, wqkv_ref[...]) + bqkv_ref[...]
        cos = cos_ref[r0:r0 + SWA_PART, :]
        sin = sin_ref[r0:r0 + SWA_PART, :]
        q = (_rope_apply(qkv[:, :SWA_Q], cos, sin) * (d ** -0.5)).astype(BF16)
        k = _rope_apply(qkv[:, SWA_Q:SWA_Q + SWA_KV], cos, sin).astype(BF16)
        v = qkv[:, SWA_Q + SWA_KV:].astype(BF16)
        dst_rows = slice(win + r0, win + r0 + SWA_PART)
        for src, dst in ((k, kpad_ref), (v, vpad_ref)):
            for pair in range(SWA_KV // LANES):
                col = src[:, pair * LANES:(pair + 1) * LANES]
                swapped = jnp.concatenate([col[:, d:], col[:, :d]], axis=1)
                zero = jnp.zeros_like(col)
                dst[4 * pair + 0, dst_rows, :] = jnp.where(low_half, col, zero)
                dst[4 * pair + 1, dst_rows, :] = jnp.where(low_half, zero, swapped)
                dst[4 * pair + 2, dst_rows, :] = jnp.where(low_half, swapped, zero)
                dst[4 * pair + 3, dst_rows, :] = jnp.where(low_half, zero, col)
        return q

    items = [(kv, par) for kv in range(SWA_KV_HEADS) for par in range(2)]

    def qk_stage(q, blk):
        lr = (blk * win) % SWA_PART
        out = {}
        for kv, par in items:
            c0 = kv * grp * d
            q2 = jnp.concatenate([q[lr:lr + win, c0:c0 + LANES], q[lr:lr + win, c0 + LANES:c0 + 2 * LANES]], axis=0)
            out[kv, par] = _dot_nt(q2, kpad_ref[2 * kv + par, blk * win:(blk + 2) * win, :])
        return out

    qi = lax.broadcasted_iota(jnp.int32, (2 * win, win), 0) % win
    kj = lax.broadcasted_iota(jnp.int32, (2 * win, win), 1)
    from_prev = kj > qi
    srow = lax.broadcasted_iota(jnp.int32, (2 * win, 1), 0)
    prev_bias = jnp.where(first, NEG_INF, 0.0)

    def softmax_pv_stage(scores, blk):
        probs = {}
        for kv, par in items:
            s = scores[kv, par]
            s_prev = s[:, :win] + prev_bias if blk == 0 else s[:, :win]
            folded = jnp.where(from_prev, s_prev, s[:, win:])
            sink = jnp.where(srow < win, sinks_ref[kv * grp + par], sinks_ref[kv * grp + 2 + par])
            mx = jnp.maximum(jnp.max(folded, axis=-1, keepdims=True), sink)
            e = jnp.exp(folded - mx)
            inv_den = 1.0 / (jnp.sum(e, axis=-1, keepdims=True) + jnp.exp(sink - mx))
            p = (e * inv_den).astype(BF16)
            zero = jnp.zeros_like(p)
            probs[kv, par] = jnp.concatenate([jnp.where(from_prev, p, zero), jnp.where(from_prev, zero, p)], axis=1)
        for kv in range(SWA_KV_HEADS):
            c0 = kv * grp * d
            out = (_dot(probs[kv, 0], vpad_ref[2 * kv, blk * win:(blk + 2) * win, :])
                   + _dot(probs[kv, 1], vpad_ref[2 * kv + 1, blk * win:(blk + 2) * win, :]))
            attn_ref[blk * win:(blk + 1) * win, c0:c0 + LANES] = out[:win].astype(BF16)
            attn_ref[blk * win:(blk + 1) * win, c0 + LANES:c0 + 2 * LANES] = out[win:].astype(BF16)

    n_parts = rows // SWA_PART
    blk_per_part = SWA_PART // win
    q = project(0)
    scores = {blk: qk_stage(q, blk) for blk in range(blk_per_part)}
    for part in range(n_parts):
        nxt = part + 1
        if nxt < n_parts:
            q = project(nxt)
        for blk in range(part * blk_per_part, nxt * blk_per_part):
            softmax_pv_stage(scores.pop(blk), blk)
        if nxt < n_parts:
            for blk in range(nxt * blk_per_part, (nxt + 1) * blk_per_part):
                scores[blk] = qk_stage(q, blk)
        rs = slice(part * SWA_PART, nxt * SWA_PART)
        o_ref[0, rs, :] = x_ref[0, rs, :] + _dot(attn_ref[rs, :], wo_ref[...]) + bo_ref[...]

    kpad_ref[:, 0:win, :] = kpad_ref[:, rows:rows + win, :]
    vpad_ref[:, 0:win, :] = vpad_ref[:, rows:rows + win, :]


def _swa(x, norm_g, w_qkv16, b_qkv, sinks, w_o16, b_o):
    b, t, _ = x.shape
    assert t % SWA_ROWS == 0 and SWA_ROWS % SWA_PART == 0 and SWA_PART % SWA_WINDOW == 0
    cos_t, sin_t = _rope_tables(t)
    n_qkv = SWA_Q + 2 * SWA_KV
    tile = pl.BlockSpec((1, SWA_ROWS, D_MODEL), lambda bi, ti, *_: (bi, ti, 0))
    table = pl.BlockSpec((SWA_ROWS, LANES), lambda bi, ti, *_: (ti, 0))
    pad_shape = (2 * SWA_KV_HEADS, SWA_WINDOW + SWA_ROWS, LANES)
    return pl.pallas_call(
        _swa_body,
        grid_spec=pltpu.PrefetchScalarGridSpec(
            num_scalar_prefetch=1,
            grid=(b, t // SWA_ROWS),
            in_specs=[tile, _resident((1, D_MODEL)), _resident((D_MODEL, n_qkv)), _resident((1, n_qkv)),
                      table, table, _resident((SWA_Q, D_MODEL)), _resident((1, D_MODEL))],
            out_specs=tile,
            scratch_shapes=[pltpu.VMEM(pad_shape, BF16), pltpu.VMEM(pad_shape, BF16),
                            pltpu.VMEM((SWA_ROWS, SWA_Q), BF16)]),
        out_shape=jax.ShapeDtypeStruct(x.shape, F32),
        compiler_params=pltpu.CompilerParams(dimension_semantics=("arbitrary", "arbitrary"),
                                             vmem_limit_bytes=VMEM_LIMIT_BYTES),
        name="swa_mixer",
    )(sinks.astype(F32), x, norm_g, w_qkv16, b_qkv.reshape(1, n_qkv).astype(F32), cos_t, sin_t,
      w_o16, b_o.reshape(1, D_MODEL).astype(F32))


def kernel(x, ffn1_norm, ffn1_w_gate, ffn1_w_up, ffn1_w_down, mix_norm, ffn2_norm, ffn2_w_gate, ffn2_w_up,
           ffn2_w_down, hyb_w_in, gdn_conv_w, gdn_a_log, gdn_dt_bias, gdn_out_norm, sc_conv_w, hyb_w_out,
           swa_w_qkv, swa_b_qkv, swa_sinks, swa_w_o, swa_b_o, final_norm):
    b, t, d = x.shape
    depth = ffn1_norm.shape[0]
    final_g = final_norm.reshape(1, d).astype(F32)
    ffn1 = [w.astype(BF16) for w in (ffn1_w_gate, ffn1_w_up, ffn1_w_down)]
    ffn2 = [w.astype(BF16) for w in (ffn2_w_gate, ffn2_w_up, ffn2_w_down)]
    hyb_in16, hyb_out16 = hyb_w_in.astype(BF16), hyb_w_out.astype(BF16)
    swa_qkv16, swa_o16 = swa_w_qkv.astype(BF16), swa_w_o.astype(BF16)

    def ffn(xin, norm, ws, li, last):
        y = _ffn(xin.reshape(b * t, d), norm.reshape(1, d).astype(F32), ws[0], ws[1], ws[2], li, final_g, last)
        return y.reshape(b, t, d)

    for li in range(depth):
        x = ffn(x, ffn1_norm[li], ffn1, li, False)
        norm = mix_norm[li].reshape(1, d).astype(F32)
        j = li // 2
        if li % 2 == 0:
            x = _hybrid(x, norm, hyb_in16[j], gdn_conv_w[j], gdn_a_log[j], gdn_dt_bias[j], gdn_out_norm[j],
                        sc_conv_w[j], hyb_out16[j])
        else:
            x = _swa(x, norm, swa_qkv16[j], swa_b_qkv[j], swa_sinks[j], swa_o16[j], swa_b_o[j])
        x = ffn(x, ffn2_norm[li], ffn2, li, li == depth - 1)
    return x
```

```python
import functools

import jax
import jax.numpy as jnp
from jax import lax
from jax.experimental import pallas as pl
from jax.experimental.pallas import tpu as pltpu

F32 = jnp.float32
BF16 = jnp.bfloat16

D_MODEL = 1024
D_FF = 2816
GDN_HEADS = 4
GDN_HEAD_DIM = 128
GDN_CONV = 4
GDN_QKV = 3 * GDN_HEADS * GDN_HEAD_DIM
GDN_Z = GDN_HEADS * GDN_HEAD_DIM
SC_CHANNELS = 512
SC_CONV = 3
SWA_HEADS = 16
SWA_KV_HEADS = 4
SWA_HEAD_DIM = 64
SWA_WINDOW = 128
SWA_Q = SWA_HEADS * SWA_HEAD_DIM
SWA_KV = SWA_KV_HEADS * SWA_HEAD_DIM
ROPE_THETA = 10000.0
RMS_EPS = 1e-6
L2_EPS = 1e-6

LANES = 128
SUBLANES = 8
VMEM_LIMIT_BYTES = 56 * 1024 * 1024

FFN_ROWS = 1024
FFN_PARTS = 4
FFN_W_CHUNKS = 8
FFN_COLS = 256
HYB_ROWS = 512
HYB_OUT_COLS = 256
HYB_ZSC_COLS = 256
HYB_ZSC_EARLY = 8
HYB_ZSC_MID = 0
GDN_UNIT = 128
SWA_ROWS = 1024
SWA_PART = 256
CONV_HALO = SUBLANES

NEG_INF = float("-inf")


def _rms(x, g):
    ms = jnp.mean(x * x, axis=-1, keepdims=True)
    return x * lax.rsqrt(ms + RMS_EPS) * g


def _sigmoid(x):
    return 1.0 / (1.0 + jnp.exp(-x))


def _silu(x):
    return x * _sigmoid(x)


def _dot(a, b):
    return jnp.dot(a, b, preferred_element_type=F32)


def _dot_nt(a, b):
    return lax.dot_general(a, b, (((1,), (1,)), ((), ())), preferred_element_type=F32)


def _dot_tn(a, b):
    return lax.dot_general(a, b, (((0,), (0,)), ((), ())), preferred_element_type=F32)


def _resident(shape):
    nd = len(shape)
    return pl.BlockSpec(shape, lambda *_: (0,) * nd, pipeline_mode=pl.Buffered(1))


def _resident_layer(shape, layer):
    nd = len(shape)
    return pl.BlockSpec((pl.squeezed,) + tuple(shape), lambda *_: (layer,) + (0,) * nd,
                        pipeline_mode=pl.Buffered(1))


def _load_cast(src_hbm, dst_ref, stage_ref, sem_ref):
    chunk = stage_ref.shape[1]
    n_chunks = src_hbm.shape[0] // chunk
    assert n_chunks * chunk == src_hbm.shape[0]

    def copy(k, slot):
        rows = pl.ds(pl.multiple_of(k * chunk, chunk), chunk)
        return pltpu.make_async_copy(src_hbm.at[rows, :], stage_ref.at[slot], sem_ref.at[slot])

    copy(0, 0).start()

    def body(k, carry):
        slot = k % 2

        @pl.when(k + 1 < n_chunks)
        def _():
            copy(k + 1, 1 - slot).start()

        copy(k, slot).wait()
        dst_ref[pl.ds(pl.multiple_of(k * chunk, chunk), chunk), :] = stage_ref[slot].astype(BF16)
        return carry

    lax.fori_loop(0, n_chunks, body, 0)


def _ffn_body(x_ref, g_ref, wg_hbm, wu_hbm, wd_hbm, fg_ref, o_ref, act_ref, wg_ref, wu_ref, wd_ref,
              stage_in_ref, stage_out_ref, sem_ref, *, layer, final_norm):
    @pl.when(pl.program_id(0) == 0)
    def _():
        _load_cast(wg_hbm.at[layer], wg_ref, stage_in_ref, sem_ref)
        _load_cast(wu_hbm.at[layer], wu_ref, stage_in_ref, sem_ref)
        _load_cast(wd_hbm.at[layer], wd_ref, stage_out_ref, sem_ref)

    part = x_ref.shape[0] // FFN_PARTS
    halves = [slice(i * part, (i + 1) * part) for i in range(FFN_PARTS)]
    hs = [_rms(x_ref[rs, :], g_ref[...]).astype(BF16) for rs in halves]
    for c in range(D_FF // FFN_COLS):
        cols = slice(c * FFN_COLS, (c + 1) * FFN_COLS)
        for h, rs in zip(hs, halves):
            gate = _dot(h, wg_ref[:, cols])
            up = _dot(h, wu_ref[:, cols])
            act_ref[rs, cols] = (_silu(gate) * up).astype(BF16)
    for rs in halves:
        y = x_ref[rs, :] + 0.5 * _dot(act_ref[rs, :], wd_ref[...])
        if final_norm:
            y = _rms(y, fg_ref[...])
        o_ref[rs, :] = y


def _ffn(x2d, norm_g, wg, wu, wd, layer, final_g, final_norm):
    rows = x2d.shape[0]
    assert rows % FFN_ROWS == 0
    row_spec = pl.BlockSpec((FFN_ROWS, D_MODEL), lambda i: (i, 0))
    hbm = pl.BlockSpec(memory_space=pl.ANY)
    return pl.pallas_call(
        functools.partial(_ffn_body, layer=layer, final_norm=final_norm),
        grid=(rows // FFN_ROWS,),
        in_specs=[row_spec, _resident((1, D_MODEL)), hbm, hbm, hbm, _resident((1, D_MODEL))],
        out_specs=row_spec,
        out_shape=jax.ShapeDtypeStruct((rows, D_MODEL), F32),
        scratch_shapes=[pltpu.VMEM((FFN_ROWS, D_FF), BF16),
                        pltpu.VMEM((D_MODEL, D_FF), BF16), pltpu.VMEM((D_MODEL, D_FF), BF16),
                        pltpu.VMEM((D_FF, D_MODEL), BF16),
                        pltpu.VMEM((2, D_MODEL // FFN_W_CHUNKS, D_FF), F32),
                        pltpu.VMEM((2, D_FF // FFN_W_CHUNKS, D_MODEL), F32),
                        pltpu.SemaphoreType.DMA((2,))],
        compiler_params=pltpu.CompilerParams(dimension_semantics=("arbitrary",),
                                             vmem_limit_bytes=VMEM_LIMIT_BYTES),
        name="ffn_final" if final_norm else "ffn",
    )(x2d, norm_g, wg, wu, wd, final_g)


def _causal_conv(ext_ref, cur, w_ref, width, cs):
    rows = cur.shape[0]
    ext_ref[CONV_HALO:CONV_HALO + rows, cs] = cur
    acc = cur * w_ref[width - 1:width, cs]
    for j in range(width - 1):
        back = width - 1 - j
        acc = acc + ext_ref[CONV_HALO - back:CONV_HALO - back + rows, cs] * w_ref[j:j + 1, cs]
    return acc


def _conv_carry(ext_ref):
    rows = ext_ref.shape[0] - CONV_HALO
    ext_ref[0:CONV_HALO, :] = ext_ref[rows:rows + CONV_HALO, :]


def _segment_cumsum(x, seg):
    pos = lax.broadcasted_iota(jnp.int32, x.shape, 0) % seg
    shift = 1
    while shift < seg:
        x = x + jnp.where(pos >= shift, pltpu.roll(x, shift, axis=0), 0.0)
        shift *= 2
    return x


def _hyb_body(x_ref, g_ref, wqkv_ref, wzsc_ref, wba_ref, convw_ref, alog_ref, dtb_ref, onorm_ref, scw_ref,
              wout_ref, o_ref, ext_ref, ext2_ref, state_ref):
    first = pl.program_id(1) == 0

    @pl.when(first)
    def _():
        state_ref[...] = jnp.zeros(state_ref.shape, F32)
        ext_ref[0:CONV_HALO, :] = jnp.zeros((CONV_HALO, ext_ref.shape[1]), F32)
        ext2_ref[0:CONV_HALO, :] = jnp.zeros((CONV_HALO, ext2_ref.shape[1]), F32)

    rows = x_ref.shape[1]
    hd = GDN_HEAD_DIM
    unit = GDN_UNIT
    x = x_ref[0]
    h = _rms(x, g_ref[...]).astype(BF16)
    p_ba = _dot(h, wba_ref[...])
    p_qkv = _dot(h, wqkv_ref[...])
    zsc = {}

    def project_zsc(piece):
        cs = slice(piece * HYB_ZSC_COLS, (piece + 1) * HYB_ZSC_COLS)
        zsc[piece] = _dot(h, wzsc_ref[:, cs])

    for piece in range(HYB_ZSC_EARLY):
        project_zsc(piece)

    def zsc_cols(rs, start, width):
        piece, off = divmod(start, HYB_ZSC_COLS)
        assert off + width <= HYB_ZSC_COLS
        return zsc[piece][rs, off:off + width]

    beta_all = _sigmoid(p_ba)
    sp_in = p_ba + dtb_ref[...]
    softplus = jnp.maximum(sp_in, 0.0) + jnp.log1p(jnp.exp(-jnp.abs(sp_in)))
    g_all = -jnp.exp(alog_ref[...]) * softplus
    gc_all = _segment_cumsum(g_all, unit)
    gc_rows = gc_all.T

    cols = []
    for cg in range(GDN_QKV // hd):
        cs = slice(cg * hd, (cg + 1) * hd)
        c = _silu(_causal_conv(ext_ref, p_qkv[:, cs], convw_ref, GDN_CONV, cs))
        if cg < 2 * GDN_HEADS:
            c = c * lax.rsqrt(jnp.sum(c * c, axis=-1, keepdims=True) + L2_EPS)
        if cg < GDN_HEADS:
            c = c * (hd ** -0.5)
        cols.append(c)

    row = lax.broadcasted_iota(jnp.int32, (unit, unit), 0)
    col = lax.broadcasted_iota(jnp.int32, (unit, unit), 1)
    lower_incl = row >= col
    lower_strict = row > col
    eye = jnp.where(row == col, 1.0, 0.0)

    n_units = rows // unit
    pairs = [(u, hh) for u in range(n_units) for hh in range(GDN_HEADS)]
    pre = {}
    for u, hh in pairs:
        rs = slice(u * unit, (u + 1) * unit)
        q = cols[hh][rs]
        k = cols[GDN_HEADS + hh][rs]
        v = cols[2 * GDN_HEADS + hh][rs]
        beta = beta_all[rs, hh:hh + 1]
        gc = gc_all[rs, GDN_HEADS + hh:GDN_HEADS + hh + 1]
        gc_row = gc_rows[GDN_HEADS + hh:GDN_HEADS + hh + 1, rs]
        g_last = gc[unit - 1:unit, :]
        e_gc = jnp.exp(gc)
        kb = k * beta
        pre[u, hh] = dict(
            decay=jnp.exp(jnp.where(lower_incl, gc - gc_row, NEG_INF)),
            kbq16=jnp.concatenate([kb, q], axis=0).astype(BF16),
            k16=k.astype(BF16),
            rhs16=jnp.concatenate([v * beta, kb * e_gc], axis=-1).astype(BF16),
            qg16=(q * e_gc).astype(BF16),
            kg16=(k * jnp.exp(g_last - gc)).astype(BF16),
            s_decay=jnp.exp(g_last))

    kq = {p: _dot_nt(pre[p]["kbq16"], pre[p]["k16"]) for p in pairs}
    neg_m, attn16, inv = {}, {}, {}
    for p in pairs:
        decay = pre[p]["decay"]
        n = jnp.where(lower_strict, -(kq[p][:unit] * decay), 0.0)
        neg_m[p] = n.astype(BF16)
        attn16[p] = (kq[p][unit:] * decay).astype(BF16)
        inv[p] = eye + n
    for piece in range(HYB_ZSC_EARLY, HYB_ZSC_EARLY + HYB_ZSC_MID):
        project_zsc(piece)
    span = 2
    while span < unit:
        resid = {p: ((eye - inv[p]) + _dot(neg_m[p], inv[p].astype(BF16))).astype(BF16) for p in pairs}
        inv = {p: inv[p] + _dot(inv[p].astype(BF16), resid[p]) for p in pairs}
        span *= 2

    sol, wq, v16, gdn_out, y_sc16 = {}, {}, {}, {}, {}
    all_rows = slice(0, rows)

    def short_conv():
        y_parts = []
        for cg in range(SC_CHANNELS // LANES):
            cs = slice(cg * LANES, (cg + 1) * LANES)
            sc_b = zsc_cols(all_rows, GDN_Z + cg * LANES, LANES)
            sc_ch = (zsc_cols(all_rows, GDN_Z + SC_CHANNELS + cg * LANES, LANES)
                     * zsc_cols(all_rows, GDN_Z + 2 * SC_CHANNELS + cg * LANES, LANES))
            y_parts.append(sc_b * _causal_conv(ext2_ref, sc_ch, scw_ref, SC_CONV, cs))
        y_sc16[0] = jnp.concatenate(y_parts, axis=-1).astype(BF16)

    def solve(u):
        for hh in range(GDN_HEADS):
            sol[u, hh] = _dot(inv[u, hh].astype(BF16), pre[u, hh]["rhs16"])

    def chain_read(u):
        for hh in range(GDN_HEADS):
            p = (u, hh)
            s16 = state_ref[hh].astype(BF16)
            wq[p] = _dot(jnp.concatenate([sol[p][:, hd:].astype(BF16), pre[p]["qg16"]], axis=0), s16)

    def chain_write(u):
        for hh in range(GDN_HEADS):
            p = (u, hh)
            v16[p] = (sol[p][:, :hd] - wq[p][:unit]).astype(BF16)
            state_ref[hh] = state_ref[hh] * pre[p]["s_decay"] + _dot_tn(pre[p]["kg16"], v16[p])

    def unit_output(u):
        rs = slice(u * unit, (u + 1) * unit)
        heads = []
        for hh in range(GDN_HEADS):
            p = (u, hh)
            o = wq[p][unit:] + _dot(attn16[p], v16[p])
            o = o * lax.rsqrt(jnp.mean(o * o, axis=-1, keepdims=True) + RMS_EPS)
            heads.append(o * onorm_ref[...] * _silu(zsc_cols(rs, hh * hd, hd)))
        gdn_out[u] = jnp.concatenate(heads, axis=-1).astype(BF16)

    def project_out(u0, col):
        rs = slice(u0 * unit, (u0 + 2) * unit)
        mix = jnp.concatenate([jnp.concatenate([gdn_out[u0], gdn_out[u0 + 1]], axis=0), y_sc16[0][rs]], axis=-1)
        cs = slice(col, col + HYB_OUT_COLS)
        o_ref[0, rs, cs] = x_ref[0, rs, cs] + _dot(mix, wout_ref[:, cs])

    assert n_units % 2 == 0
    late_pieces = [functools.partial(project_zsc, piece)
                   for piece in range(HYB_ZSC_EARLY + HYB_ZSC_MID, (GDN_Z + 3 * SC_CHANNELS) // HYB_ZSC_COLS)]
    fillers = []
    for u in range(1, n_units):
        fillers.append(functools.partial(solve, u))
        if late_pieces:
            fillers.append(late_pieces.pop(0))
    fillers += late_pieces + [short_conv]
    solve(0)
    for u in range(n_units):
        if (u, 0) not in sol:
            fillers = [f for f in fillers if not (f.func is solve and f.args == (u,))]
            solve(u)
        for step in (chain_read, chain_write):
            step(u)
            if fillers:
                fillers.pop(0)()
        fillers.append(functools.partial(unit_output, u))
        if u % 2 == 1:
            fillers += [functools.partial(project_out, u - 1, col) for col in range(0, D_MODEL, HYB_OUT_COLS)]
    for f in fillers:
        f()
    _conv_carry(ext_ref)
    _conv_carry(ext2_ref)


def _hybrid(x, norm_g, w_in16, conv_w, a_log, dt_bias, out_norm, sc_conv_w, w_out16):
    b, t, _ = x.shape
    assert t % HYB_ROWS == 0 and HYB_ROWS % GDN_UNIT == 0
    n_qkvz = GDN_QKV + GDN_Z
    n_zsc = GDN_Z + 3 * SC_CHANNELS
    w_qkv = w_in16[:, :GDN_QKV]
    w_ba = jnp.pad(w_in16[:, n_qkvz:n_qkvz + 2 * GDN_HEADS], ((0, 0), (0, LANES - 2 * GDN_HEADS)))
    w_zsc = jnp.concatenate([w_in16[:, GDN_QKV:n_qkvz], w_in16[:, n_qkvz + 2 * GDN_HEADS:]], axis=1)
    lane_pad = (GDN_HEADS, LANES - 2 * GDN_HEADS)
    alog_row = jnp.pad(a_log.astype(F32), lane_pad).reshape(1, LANES)
    dtb_row = jnp.pad(dt_bias.astype(F32), lane_pad).reshape(1, LANES)
    tile = pl.BlockSpec((1, HYB_ROWS, D_MODEL), lambda bi, ti: (bi, ti, 0))
    return pl.pallas_call(
        _hyb_body,
        grid=(b, t // HYB_ROWS),
        in_specs=[tile, _resident((1, D_MODEL)), _resident((D_MODEL, GDN_QKV)), _resident((D_MODEL, n_zsc)),
                  _resident((D_MODEL, LANES)), _resident((GDN_CONV, GDN_QKV)), _resident((1, LANES)),
                  _resident((1, LANES)), _resident((1, GDN_HEAD_DIM)), _resident((SC_CONV, SC_CHANNELS)),
                  _resident((GDN_Z + SC_CHANNELS, D_MODEL))],
        out_specs=tile,
        out_shape=jax.ShapeDtypeStruct(x.shape, F32),
        scratch_shapes=[pltpu.VMEM((CONV_HALO + HYB_ROWS, GDN_QKV), F32),
                        pltpu.VMEM((CONV_HALO + HYB_ROWS, SC_CHANNELS), F32),
                        pltpu.VMEM((GDN_HEADS, GDN_HEAD_DIM, GDN_HEAD_DIM), F32)],
        compiler_params=pltpu.CompilerParams(dimension_semantics=("arbitrary", "arbitrary"),
                                             vmem_limit_bytes=VMEM_LIMIT_BYTES),
        name="hybrid_mixer",
    )(x, norm_g, w_qkv, w_zsc, w_ba, conv_w.astype(F32), alog_row, dtb_row,
      out_norm.reshape(1, GDN_HEAD_DIM).astype(F32), sc_conv_w.astype(F32), w_out16)


def _rope_body(inv_ref, cos_ref, sin_ref):
    rows = cos_ref.shape[0]
    half = SWA_HEAD_DIM // 2
    t = lax.broadcasted_iota(jnp.int32, (rows, LANES), 0) + pl.program_id(0) * rows
    lane = lax.broadcasted_iota(jnp.int32, (rows, LANES), 1)
    ang = t.astype(F32) * inv_ref[...]
    cos_ref[...] = jnp.cos(ang)
    sin_ref[...] = jnp.where(lane % SWA_HEAD_DIM < half, -1.0, 1.0) * jnp.sin(ang)


def _rope_tables(t):
    half = SWA_HEAD_DIM // 2
    inv = ROPE_THETA ** (-jnp.arange(half, dtype=F32) / half)
    inv_row = jnp.tile(inv, LANES // half).reshape(1, LANES)
    rows = min(t, 1024)
    assert t % rows == 0
    spec = pl.BlockSpec((rows, LANES), lambda i: (i, 0))
    return pl.pallas_call(
        _rope_body,
        grid=(t // rows,),
        in_specs=[pl.BlockSpec((1, LANES), lambda i: (0, 0))],
        out_specs=[spec, spec],
        out_shape=[jax.ShapeDtypeStruct((t, LANES), F32)] * 2,
        compiler_params=pltpu.CompilerParams(dimension_semantics=("arbitrary",)),
        name="rope_tables",
    )(inv_row)


def _rope_apply(x, cos, sin_signed):
    half = SWA_HEAD_DIM // 2
    width = x.shape[1]
    lane = lax.broadcasted_iota(jnp.int32, x.shape, 1)
    swapped = jnp.where(lane % SWA_HEAD_DIM < half,
                        pltpu.roll(x, width - half, axis=1),
                        pltpu.roll(x, half, axis=1))
    reps = width // LANES
    cos_w = jnp.concatenate([cos] * reps, axis=1)
    sin_w = jnp.concatenate([sin_signed] * reps, axis=1)
    return x * cos_w + swapped * sin_w


def _swa_body(sinks_ref, x_ref, g_ref, wqkv_ref, bqkv_ref, cos_ref, sin_ref, wo_ref, bo_ref, o_ref,
              kpad_ref, vpad_ref, attn_ref):
    first = pl.program_id(1) == 0
    rows = x_ref.shape[1]
    win = SWA_WINDOW
    d = SWA_HEAD_DIM
    grp = SWA_HEADS // SWA_KV_HEADS
    n_items = 2 * SWA_KV_HEADS

    @pl.when(first)
    def _():
        kpad_ref[:, 0:win, :] = jnp.zeros((n_items, win, LANES), BF16)
        vpad_ref[:, 0:win, :] = jnp.zeros((n_items, win, LANES), BF16)

    low_half = lax.broadcasted_iota(jnp.int32, (SWA_PART, LANES), 1) < d

    def project(part):
        r0 = part * SWA_PART
        h = _rms(x_ref[0, r0:r0 + SWA_PART, :], g_ref[...]).astype(BF16)
        qkv = _dot(h, wqkv_ref[...]) + bqkv_ref[...]
        cos = cos_ref[r0:r0 + SWA_PART, :]
        sin = sin_ref[r0:r0 + SWA_PART, :]
        q = (_rope_apply(qkv[:, :SWA_Q], cos, sin) * (d ** -0.5)).astype(BF16)
        k = _rope_apply(qkv[:, SWA_Q:SWA_Q + SWA_KV], cos, sin).astype(BF16)
        v = qkv[:, SWA_Q + SWA_KV:].astype(BF16)
        dst_rows = slice(win + r0, win + r0 + SWA_PART)
        for src, dst in ((k, kpad_ref), (v, vpad_ref)):
            for pair in range(SWA_KV // LANES):
                col = src[:, pair * LANES:(pair + 1) * LANES]
                swapped = jnp.concatenate([col[:, d:], col[:, :d]], axis=1)
                zero = jnp.zeros_like(col)
                dst[4 * pair + 0, dst_rows, :] = jnp.where(low_half, col, zero)
                dst[4 * pair + 1, dst_rows, :] = jnp.where(low_half, zero, swapped)
                dst[4 * pair + 2, dst_rows, :] = jnp.where(low_half, swapped, zero)
                dst[4 * pair + 3, dst_rows, :] = jnp.where(low_half, zero, col)
        return q

    items = [(kv, par) for kv in range(SWA_KV_HEADS) for par in range(2)]

    def qk_stage(q, blk):
        lr = (blk * win) % SWA_PART
        out = {}
        for kv, par in items:
            c0 = kv * grp * d
            q2 = jnp.concatenate([q[lr:lr + win, c0:c0 + LANES], q[lr:lr + win, c0 + LANES:c0 + 2 * LANES]], axis=0)
            out[kv, par] = _dot_nt(q2, kpad_ref[2 * kv + par, blk * win:(blk + 2) * win, :])
        return out

    qi = lax.broadcasted_iota(jnp.int32, (2 * win, win), 0) % win
    kj = lax.broadcasted_iota(jnp.int32, (2 * win, win), 1)
    from_prev = kj > qi
    srow = lax.broadcasted_iota(jnp.int32, (2 * win, 1), 0)
    prev_bias = jnp.where(first, NEG_INF, 0.0)

    def softmax_pv_stage(scores, blk):
        probs = {}
        for kv, par in items:
            s = scores[kv, par]
            s_prev = s[:, :win] + prev_bias if blk == 0 else s[:, :win]
            folded = jnp.where(from_prev, s_prev, s[:, win:])
            sink = jnp.where(srow < win, sinks_ref[kv * grp + par], sinks_ref[kv * grp + 2 + par])
            mx = jnp.maximum(jnp.max(folded, axis=-1, keepdims=True), sink)
            e = jnp.exp(folded - mx)
            inv_den = 1.0 / (jnp.sum(e, axis=-1, keepdims=True) + jnp.exp(sink - mx))
            p = (e * inv_den).astype(BF16)
            zero = jnp.zeros_like(p)
            probs[kv, par] = jnp.concatenate([jnp.where(from_prev, p, zero), jnp.where(from_prev, zero, p)], axis=1)
        for kv in range(SWA_KV_HEADS):
            c0 = kv * grp * d
            out = (_dot(probs[kv, 0], vpad_ref[2 * kv, blk * win:(blk + 2) * win, :])
                   + _dot(probs[kv, 1], vpad_ref[2 * kv + 1, blk * win:(blk + 2) * win, :]))
            attn_ref[blk * win:(blk + 1) * win, c0:c0 + LANES] = out[:win].astype(BF16)
            attn_ref[blk * win:(blk + 1) * win, c0 + LANES:c0 + 2 * LANES] = out[win:].astype(BF16)

    n_parts = rows // SWA_PART
    blk_per_part = SWA_PART // win
    q = project(0)
    scores = {blk: qk_stage(q, blk) for blk in range(blk_per_part)}
    for part in range(n_parts):
        nxt = part + 1
        if nxt < n_parts:
            q = project(nxt)
        for blk in range(part * blk_per_part, nxt * blk_per_part):
            softmax_pv_stage(scores.pop(blk), blk)
        if nxt < n_parts:
            for blk in range(nxt * blk_per_part, (nxt + 1) * blk_per_part):
                scores[blk] = qk_stage(q, blk)
        rs = slice(part * SWA_PART, nxt * SWA_PART)
        o_ref[0, rs, :] = x_ref[0, rs, :] + _dot(attn_ref[rs, :], wo_ref[...]) + bo_ref[...]

    kpad_ref[:, 0:win, :] = kpad_ref[:, rows:rows + win, :]
    vpad_ref[:, 0:win, :] = vpad_ref[:, rows:rows + win, :]


def _swa(x, norm_g, w_qkv16, b_qkv, sinks, w_o16, b_o):
    b, t, _ = x.shape
    assert t % SWA_ROWS == 0 and SWA_ROWS % SWA_PART == 0 and SWA_PART % SWA_WINDOW == 0
    cos_t, sin_t = _rope_tables(t)
    n_qkv = SWA_Q + 2 * SWA_KV
    tile = pl.BlockSpec((1, SWA_ROWS, D_MODEL), lambda bi, ti, *_: (bi, ti, 0))
    table = pl.BlockSpec((SWA_ROWS, LANES), lambda bi, ti, *_: (ti, 0))
    pad_shape = (2 * SWA_KV_HEADS, SWA_WINDOW + SWA_ROWS, LANES)
    return pl.pallas_call(
        _swa_body,
        grid_spec=pltpu.PrefetchScalarGridSpec(
            num_scalar_prefetch=1,
            grid=(b, t // SWA_ROWS),
            in_specs=[tile, _resident((1, D_MODEL)), _resident((D_MODEL, n_qkv)), _resident((1, n_qkv)),
                      table, table, _resident((SWA_Q, D_MODEL)), _resident((1, D_MODEL))],
            out_specs=tile,
            scratch_shapes=[pltpu.VMEM(pad_shape, BF16), pltpu.VMEM(pad_shape, BF16),
                            pltpu.VMEM((SWA_ROWS, SWA_Q), BF16)]),
        out_shape=jax.ShapeDtypeStruct(x.shape, F32),
        compiler_params=pltpu.CompilerParams(dimension_semantics=("arbitrary", "arbitrary"),
                                             vmem_limit_bytes=VMEM_LIMIT_BYTES),
        name="swa_mixer",
    )(sinks.astype(F32), x, norm_g, w_qkv16, b_qkv.reshape(1, n_qkv).astype(F32), cos_t, sin_t,
      w_o16, b_o.reshape(1, D_MODEL).astype(F32))


def kernel(x, ffn1_norm, ffn1_w_gate, ffn1_w_up, ffn1_w_down, mix_norm, ffn2_norm, ffn2_w_gate, ffn2_w_up,
           ffn2_w_down, hyb_w_in, gdn_conv_w, gdn_a_log, gdn_dt_bias, gdn_out_norm, sc_conv_w, hyb_w_out,
           swa_w_qkv, swa_b_qkv, swa_sinks, swa_w_o, swa_b_o, final_norm):
    b, t, d = x.shape
    depth = ffn1_norm.shape[0]
    final_g = final_norm.reshape(1, d).astype(F32)
    ffn1 = [w.astype(F32) for w in (ffn1_w_gate, ffn1_w_up, ffn1_w_down)]
    ffn2 = [w.astype(F32) for w in (ffn2_w_gate, ffn2_w_up, ffn2_w_down)]
    hyb_in16, hyb_out16 = hyb_w_in.astype(BF16), hyb_w_out.astype(BF16)
    swa_qkv16, swa_o16 = swa_w_qkv.astype(BF16), swa_w_o.astype(BF16)

    def ffn(xin, norm, ws, li, last):
        y = _ffn(xin.reshape(b * t, d), norm.reshape(1, d).astype(F32), ws[0], ws[1], ws[2], li, final_g, last)
        return y.reshape(b, t, d)

    for li in range(depth):
        x = ffn(x, ffn1_norm[li], ffn1, li, False)
        norm = mix_norm[li].reshape(1, d).astype(F32)
        j = li // 2
        if li % 2 == 0:
            x = _hybrid(x, norm, hyb_in16[j], gdn_conv_w[j], gdn_a_log[j], gdn_dt_bias[j], gdn_out_norm[j],
                        sc_conv_w[j], hyb_out16[j])
        else:
            x = _swa(x, norm, swa_qkv16[j], swa_b_qkv[j], swa_sinks[j], swa_o16[j], swa_b_o[j])
        x = ffn(x, ffn2_norm[li], ffn2, li, li == depth - 1)
    return x
```

```python
import functools

import jax
import jax.numpy as jnp
from jax import lax
from jax.experimental import pallas as pl
from jax.experimental.pallas import tpu as pltpu

F32 = jnp.float32
BF16 = jnp.bfloat16

D_MODEL = 1024
D_FF = 2816
GDN_HEADS = 4
GDN_HEAD_DIM = 128
GDN_CONV = 4
GDN_QKV = 3 * GDN_HEADS * GDN_HEAD_DIM
GDN_Z = GDN_HEADS * GDN_HEAD_DIM
SC_CHANNELS = 512
SC_CONV = 3
SWA_HEADS = 16
SWA_KV_HEADS = 4
SWA_HEAD_DIM = 64
SWA_WINDOW = 128
SWA_Q = SWA_HEADS * SWA_HEAD_DIM
SWA_KV = SWA_KV_HEADS * SWA_HEAD_DIM
ROPE_THETA = 10000.0
RMS_EPS = 1e-6
L2_EPS = 1e-6

LANES = 128
SUBLANES = 8
VMEM_LIMIT_BYTES = 56 * 1024 * 1024

FFN_ROWS = 1024
FFN_PARTS = 4
FFN_W_CHUNKS = 8
FFN_COLS = 256
HYB_ROWS = 512
HYB_OUT_COLS = 256
HYB_ZSC_COLS = 256
HYB_ZSC_EARLY = 8
HYB_ZSC_MID = 0
GDN_UNIT = 128
SWA_ROWS = 1024
SWA_PART = 256
SWA_LOOKAHEAD = 1
CONV_HALO = SUBLANES
ROPE_SPAN = 128

NEG_INF = float("-inf")


def _rms(x, g):
    ms = jnp.mean(x * x, axis=-1, keepdims=True)
    return x * lax.rsqrt(ms + RMS_EPS) * g


def _sigmoid(x):
    return 1.0 / (1.0 + jnp.exp(-x))


def _silu(x):
    return x * _sigmoid(x)


def _dot(a, b):
    return jnp.dot(a, b, preferred_element_type=F32)


def _dot_nt(a, b):
    return lax.dot_general(a, b, (((1,), (1,)), ((), ())), preferred_element_type=F32)


def _dot_tn(a, b):
    return lax.dot_general(a, b, (((0,), (0,)), ((), ())), preferred_element_type=F32)


def _resident(shape):
    nd = len(shape)
    return pl.BlockSpec(shape, lambda *_: (0,) * nd, pipeline_mode=pl.Buffered(1))


def _resident_layer(shape, layer):
    nd = len(shape)
    return pl.BlockSpec((pl.squeezed,) + tuple(shape), lambda *_: (layer,) + (0,) * nd,
                        pipeline_mode=pl.Buffered(1))


def _load_cast(src_hbm, dst_ref, stage_ref, sem_ref):
    chunk = stage_ref.shape[1]
    n_chunks = src_hbm.shape[0] // chunk
    assert n_chunks * chunk == src_hbm.shape[0]

    def copy(k, slot):
        rows = pl.ds(pl.multiple_of(k * chunk, chunk), chunk)
        return pltpu.make_async_copy(src_hbm.at[rows, :], stage_ref.at[slot], sem_ref.at[slot])

    copy(0, 0).start()

    def body(k, carry):
        slot = k % 2

        @pl.when(k + 1 < n_chunks)
        def _():
            copy(k + 1, 1 - slot).start()

        copy(k, slot).wait()
        dst_ref[pl.ds(pl.multiple_of(k * chunk, chunk), chunk), :] = stage_ref[slot].astype(BF16)
        return carry

    lax.fori_loop(0, n_chunks, body, 0)


def _ffn_body(x_ref, g_ref, wg_hbm, wu_hbm, wd_hbm, fg_ref, o_ref, act_ref, wg_ref, wu_ref, wd_ref,
              stage_in_ref, stage_out_ref, sem_ref, *, layer, final_norm):
    @pl.when(pl.program_id(0) == 0)
    def _():
        _load_cast(wg_hbm.at[layer], wg_ref, stage_in_ref, sem_ref)
        _load_cast(wu_hbm.at[layer], wu_ref, stage_in_ref, sem_ref)
        _load_cast(wd_hbm.at[layer], wd_ref, stage_out_ref, sem_ref)

    part = x_ref.shape[0] // FFN_PARTS
    halves = [slice(i * part, (i + 1) * part) for i in range(FFN_PARTS)]
    hs = [_rms(x_ref[rs, :], g_ref[...]).astype(BF16) for rs in halves]
    for c in range(D_FF // FFN_COLS):
        cols = slice(c * FFN_COLS, (c + 1) * FFN_COLS)
        for h, rs in zip(hs, halves):
            gate = _dot(h, wg_ref[:, cols])
            up = _dot(h, wu_ref[:, cols])
            act_ref[rs, cols] = (_silu(gate) * up).astype(BF16)
    for rs in halves:
        y = x_ref[rs, :] + 0.5 * _dot(act_ref[rs, :], wd_ref[...])
        if final_norm:
            y = _rms(y, fg_ref[...])
        o_ref[rs, :] = y


def _ffn(x2d, norm_g, wg, wu, wd, layer, final_g, final_norm):
    rows = x2d.shape[0]
    assert rows % FFN_ROWS == 0
    row_spec = pl.BlockSpec((FFN_ROWS, D_MODEL), lambda i: (i, 0))
    hbm = pl.BlockSpec(memory_space=pl.ANY)
    return pl.pallas_call(
        functools.partial(_ffn_body, layer=layer, final_norm=final_norm),
        grid=(rows // FFN_ROWS,),
        in_specs=[row_spec, _resident((1, D_MODEL)), hbm, hbm, hbm, _resident((1, D_MODEL))],
        out_specs=row_spec,
        out_shape=jax.ShapeDtypeStruct((rows, D_MODEL), F32),
        scratch_shapes=[pltpu.VMEM((FFN_ROWS, D_FF), BF16),
                        pltpu.VMEM((D_MODEL, D_FF), BF16), pltpu.VMEM((D_MODEL, D_FF), BF16),
                        pltpu.VMEM((D_FF, D_MODEL), BF16),
                        pltpu.VMEM((2, D_MODEL // FFN_W_CHUNKS, D_FF), F32),
                        pltpu.VMEM((2, D_FF // FFN_W_CHUNKS, D_MODEL), F32),
                        pltpu.SemaphoreType.DMA((2,))],
        compiler_params=pltpu.CompilerParams(dimension_semantics=("arbitrary",),
                                             vmem_limit_bytes=VMEM_LIMIT_BYTES),
        name="ffn_final" if final_norm else "ffn",
    )(x2d, norm_g, wg, wu, wd, final_g)


def _causal_conv(halo_ref, cur, w_ref, width, cs):
    rows = cur.shape[0]
    ext = jnp.concatenate([halo_ref[:, cs], cur], axis=0)
    halo_ref[:, cs] = cur[rows - CONV_HALO:, :]
    acc = ext * w_ref[0:1, cs]
    for j in range(1, width):
        acc = pltpu.roll(acc, 1, axis=0) + ext * w_ref[j:j + 1, cs]
    return acc[CONV_HALO:, :]


def _segment_cumsum(x, seg):
    pos = lax.broadcasted_iota(jnp.int32, x.shape, 0) % seg
    shift = 1
    while shift < seg:
        x = x + jnp.where(pos >= shift, pltpu.roll(x, shift, axis=0), 0.0)
        shift *= 2
    return x


def _hyb_body(x_ref, g_ref, wqkv_ref, wzsc_ref, wba_ref, convw_ref, alog_ref, dtb_ref, onorm_ref, scw_ref,
              wout_ref, o_ref, halo_ref, halo2_ref, state_ref):
    first = pl.program_id(1) == 0

    @pl.when(first)
    def _():
        state_ref[...] = jnp.zeros(state_ref.shape, F32)
        halo_ref[...] = jnp.zeros(halo_ref.shape, F32)
        halo2_ref[...] = jnp.zeros(halo2_ref.shape, F32)

    rows = x_ref.shape[1]
    hd = GDN_HEAD_DIM
    unit = GDN_UNIT
    x = x_ref[0]
    h = _rms(x, g_ref[...]).astype(BF16)
    p_ba = _dot(h, wba_ref[...])
    p_qkv = _dot(h, wqkv_ref[...])
    zsc = {}

    def project_zsc(piece):
        cs = slice(piece * HYB_ZSC_COLS, (piece + 1) * HYB_ZSC_COLS)
        zsc[piece] = _dot(h, wzsc_ref[:, cs])

    for piece in range(HYB_ZSC_EARLY):
        project_zsc(piece)

    def zsc_cols(rs, start, width):
        piece, off = divmod(start, HYB_ZSC_COLS)
        assert off + width <= HYB_ZSC_COLS
        return zsc[piece][rs, off:off + width]

    beta_all = _sigmoid(p_ba)
    sp_in = p_ba + dtb_ref[...]
    softplus = jnp.maximum(sp_in, 0.0) + jnp.log1p(jnp.exp(-jnp.abs(sp_in)))
    g_all = -jnp.exp(alog_ref[...]) * softplus
    gc_all = _segment_cumsum(g_all, unit)
    gc_rows = gc_all.T

    cols = []
    for cg in range(GDN_QKV // hd):
        cs = slice(cg * hd, (cg + 1) * hd)
        c = _silu(_causal_conv(halo_ref, p_qkv[:, cs], convw_ref, GDN_CONV, cs))
        if cg < 2 * GDN_HEADS:
            c = c * lax.rsqrt(jnp.sum(c * c, axis=-1, keepdims=True) + L2_EPS)
        if cg < GDN_HEADS:
            c = c * (hd ** -0.5)
        cols.append(c)

    row = lax.broadcasted_iota(jnp.int32, (unit, unit), 0)
    col = lax.broadcasted_iota(jnp.int32, (unit, unit), 1)
    lower_incl = row >= col
    lower_strict = row > col
    eye = jnp.where(row == col, 1.0, 0.0)

    n_units = rows // unit
    pairs = [(u, hh) for u in range(n_units) for hh in range(GDN_HEADS)]
    pre = {}
    for u, hh in pairs:
        rs = slice(u * unit, (u + 1) * unit)
        q = cols[hh][rs]
        k = cols[GDN_HEADS + hh][rs]
        v = cols[2 * GDN_HEADS + hh][rs]
        beta = beta_all[rs, hh:hh + 1]
        gc = gc_all[rs, GDN_HEADS + hh:GDN_HEADS + hh + 1]
        gc_row = gc_rows[GDN_HEADS + hh:GDN_HEADS + hh + 1, rs]
        g_last = gc[unit - 1:unit, :]
        e_gc = jnp.exp(gc)
        kb = k * beta
        pre[u, hh] = dict(
            decay=jnp.exp(jnp.where(lower_incl, gc - gc_row, NEG_INF)),
            kbq16=jnp.concatenate([kb, q], axis=0).astype(BF16),
            k16=k.astype(BF16),
            rhs16=jnp.concatenate([v * beta, kb * e_gc], axis=-1).astype(BF16),
            qg16=(q * e_gc).astype(BF16),
            kg16=(k * jnp.exp(g_last - gc)).astype(BF16),
            s_decay=jnp.exp(g_last))

    kq = {p: _dot_nt(pre[p]["kbq16"], pre[p]["k16"]) for p in pairs}
    neg_m, attn16, inv = {}, {}, {}
    for p in pairs:
        decay = pre[p]["decay"]
        n = jnp.where(lower_strict, -(kq[p][:unit] * decay), 0.0)
        neg_m[p] = n.astype(BF16)
        attn16[p] = (kq[p][unit:] * decay).astype(BF16)
        inv[p] = eye + n
    for piece in range(HYB_ZSC_EARLY, HYB_ZSC_EARLY + HYB_ZSC_MID):
        project_zsc(piece)
    span = 2
    while span < unit:
        resid = {p: ((eye - inv[p]) + _dot(neg_m[p], inv[p].astype(BF16))).astype(BF16) for p in pairs}
        inv = {p: inv[p] + _dot(inv[p].astype(BF16), resid[p]) for p in pairs}
        span *= 2

    sol, wq, v16, gdn_out, y_sc16 = {}, {}, {}, {}, {}
    all_rows = slice(0, rows)

    def short_conv():
        y_parts = []
        for cg in range(SC_CHANNELS // LANES):
            cs = slice(cg * LANES, (cg + 1) * LANES)
            sc_b = zsc_cols(all_rows, GDN_Z + cg * LANES, LANES)
            sc_ch = (zsc_cols(all_rows, GDN_Z + SC_CHANNELS + cg * LANES, LANES)
                     * zsc_cols(all_rows, GDN_Z + 2 * SC_CHANNELS + cg * LANES, LANES))
            y_parts.append(sc_b * _causal_conv(halo2_ref, sc_ch, scw_ref, SC_CONV, cs))
        y_sc16[0] = jnp.concatenate(y_parts, axis=-1).astype(BF16)

    def solve(u):
        for hh in range(GDN_HEADS):
            sol[u, hh] = _dot(inv[u, hh].astype(BF16), pre[u, hh]["rhs16"])

    def chain_read(u):
        for hh in range(GDN_HEADS):
            p = (u, hh)
            s16 = state_ref[hh].astype(BF16)
            wq[p] = _dot(jnp.concatenate([sol[p][:, hd:].astype(BF16), pre[p]["qg16"]], axis=0), s16)

    def chain_write(u):
        for hh in range(GDN_HEADS):
            p = (u, hh)
            v16[p] = (sol[p][:, :hd] - wq[p][:unit]).astype(BF16)
            state_ref[hh] = state_ref[hh] * pre[p]["s_decay"] + _dot_tn(pre[p]["kg16"], v16[p])

    def unit_output(u):
        rs = slice(u * unit, (u + 1) * unit)
        heads = []
        for hh in range(GDN_HEADS):
            p = (u, hh)
            o = wq[p][unit:] + _dot(attn16[p], v16[p])
            o = o * lax.rsqrt(jnp.mean(o * o, axis=-1, keepdims=True) + RMS_EPS)
            heads.append(o * onorm_ref[...] * _silu(zsc_cols(rs, hh * hd, hd)))
        gdn_out[u] = jnp.concatenate(heads, axis=-1).astype(BF16)

    def project_out(u0, col):
        rs = slice(u0 * unit, (u0 + 2) * unit)
        mix = jnp.concatenate([jnp.concatenate([gdn_out[u0], gdn_out[u0 + 1]], axis=0), y_sc16[0][rs]], axis=-1)
        cs = slice(col, col + HYB_OUT_COLS)
        o_ref[0, rs, cs] = x_ref[0, rs, cs] + _dot(mix, wout_ref[:, cs])

    assert n_units % 2 == 0
    late_pieces = [functools.partial(project_zsc, piece)
                   for piece in range(HYB_ZSC_EARLY + HYB_ZSC_MID, (GDN_Z + 3 * SC_CHANNELS) // HYB_ZSC_COLS)]
    fillers = []
    for u in range(1, n_units):
        fillers.append(functools.partial(solve, u))
        if late_pieces:
            fillers.append(late_pieces.pop(0))
    fillers += late_pieces + [short_conv]
    solve(0)
    for u in range(n_units):
        if (u, 0) not in sol:
            fillers = [f for f in fillers if not (f.func is solve and f.args == (u,))]
            solve(u)
        for step in (chain_read, chain_write):
            step(u)
            if fillers:
                fillers.pop(0)()
        fillers.append(functools.partial(unit_output, u))
        if u % 2 == 1:
            fillers += [functools.partial(project_out, u - 1, col) for col in range(0, D_MODEL, HYB_OUT_COLS)]
    for f in fillers:
        f()


def _hybrid(x, norm_g, w_in16, conv_w, a_log, dt_bias, out_norm, sc_conv_w, w_out16):
    b, t, _ = x.shape
    assert t % HYB_ROWS == 0 and HYB_ROWS % GDN_UNIT == 0
    n_qkvz = GDN_QKV + GDN_Z
    n_zsc = GDN_Z + 3 * SC_CHANNELS
    w_qkv = w_in16[:, :GDN_QKV]
    w_ba = jnp.pad(w_in16[:, n_qkvz:n_qkvz + 2 * GDN_HEADS], ((0, 0), (0, LANES - 2 * GDN_HEADS)))
    w_zsc = jnp.concatenate([w_in16[:, GDN_QKV:n_qkvz], w_in16[:, n_qkvz + 2 * GDN_HEADS:]], axis=1)
    lane_pad = (GDN_HEADS, LANES - 2 * GDN_HEADS)
    alog_row = jnp.pad(a_log.astype(F32), lane_pad).reshape(1, LANES)
    dtb_row = jnp.pad(dt_bias.astype(F32), lane_pad).reshape(1, LANES)
    tile = pl.BlockSpec((1, HYB_ROWS, D_MODEL), lambda bi, ti: (bi, ti, 0))
    return pl.pallas_call(
        _hyb_body,
        grid=(b, t // HYB_ROWS),
        in_specs=[tile, _resident((1, D_MODEL)), _resident((D_MODEL, GDN_QKV)), _resident((D_MODEL, n_zsc)),
                  _resident((D_MODEL, LANES)), _resident((GDN_CONV, GDN_QKV)), _resident((1, LANES)),
                  _resident((1, LANES)), _resident((1, GDN_HEAD_DIM)), _resident((SC_CONV, SC_CHANNELS)),
                  _resident((GDN_Z + SC_CHANNELS, D_MODEL))],
        out_specs=tile,
        out_shape=jax.ShapeDtypeStruct(x.shape, F32),
        scratch_shapes=[pltpu.VMEM((CONV_HALO, GDN_QKV), F32),
                        pltpu.VMEM((CONV_HALO, SC_CHANNELS), F32),
                        pltpu.VMEM((GDN_HEADS, GDN_HEAD_DIM, GDN_HEAD_DIM), F32)],
        compiler_params=pltpu.CompilerParams(dimension_semantics=("arbitrary", "arbitrary"),
                                             vmem_limit_bytes=VMEM_LIMIT_BYTES),
        name="hybrid_mixer",
    )(x, norm_g, w_qkv, w_zsc, w_ba, conv_w.astype(F32), alog_row, dtb_row,
      out_norm.reshape(1, GDN_HEAD_DIM).astype(F32), sc_conv_w.astype(F32), w_out16)


def _rope_body(inv_ref, cos_ref, sin_ref, cos_lo_ref, sin_lo_ref):
    rows = cos_ref.shape[0]
    half = SWA_HEAD_DIM // 2
    inv = inv_ref[...]
    lane = lax.broadcasted_iota(jnp.int32, (1, LANES), 1)
    sign = jnp.where(lane % SWA_HEAD_DIM < half, -1.0, 1.0)

    @pl.when(pl.program_id(0) == 0)
    def _():
        low = lax.broadcasted_iota(jnp.int32, (ROPE_SPAN, LANES), 0).astype(F32) * inv
        cos_lo_ref[...] = jnp.cos(low)
        sin_lo_ref[...] = jnp.sin(low)

    n_hi = rows // ROPE_SPAN
    a = lax.broadcasted_iota(jnp.int32, (n_hi, LANES), 0) + pl.program_id(0) * n_hi
    high = (a * ROPE_SPAN).astype(F32) * inv
    cos_hi, sin_hi = jnp.cos(high), jnp.sin(high)
    cos_lo, sin_lo = cos_lo_ref[...], sin_lo_ref[...]
    for i in range(n_hi):
        rs = slice(i * ROPE_SPAN, (i + 1) * ROPE_SPAN)
        c, s = cos_hi[i:i + 1, :], sin_hi[i:i + 1, :]
        cos_ref[rs, :] = c * cos_lo - s * sin_lo
        sin_ref[rs, :] = sign * (s * cos_lo + c * sin_lo)


def _rope_tables(t):
    half = SWA_HEAD_DIM // 2
    inv = ROPE_THETA ** (-jnp.arange(half, dtype=F32) / half)
    inv_row = jnp.tile(inv, LANES // half).reshape(1, LANES)
    rows = min(t, 1024)
    assert t % rows == 0 and rows % ROPE_SPAN == 0
    spec = pl.BlockSpec((rows, LANES), lambda i: (i, 0))
    return pl.pallas_call(
        _rope_body,
        grid=(t // rows,),
        in_specs=[pl.BlockSpec((1, LANES), lambda i: (0, 0))],
        out_specs=[spec, spec],
        out_shape=[jax.ShapeDtypeStruct((t, LANES), F32)] * 2,
        scratch_shapes=[pltpu.VMEM((ROPE_SPAN, LANES), F32)] * 2,
        compiler_params=pltpu.CompilerParams(dimension_semantics=("arbitrary",)),
        name="rope_tables",
    )(inv_row)


def _rope_apply(x, cos, sin_signed):
    half = SWA_HEAD_DIM // 2
    width = x.shape[1]
    lane = lax.broadcasted_iota(jnp.int32, x.shape, 1)
    swapped = jnp.where(lane % SWA_HEAD_DIM < half,
                        pltpu.roll(x, width - half, axis=1),
                        pltpu.roll(x, half, axis=1))
    reps = width // LANES
    cos_w = jnp.concatenate([cos] * reps, axis=1)
    sin_w = jnp.concatenate([sin_signed] * reps, axis=1)
    return x * cos_w + swapped * sin_w


def _swa_body(sinks_ref, x_ref, g_ref, wqkv_ref, bqkv_ref, cos_ref, sin_ref, wo_ref, bo_ref, o_ref,
              kpad_ref, vpad_ref, attn_ref):
    first = pl.program_id(1) == 0
    rows = x_ref.shape[1]
    win = SWA_WINDOW
    d = SWA_HEAD_DIM
    grp = SWA_HEADS // SWA_KV_HEADS
    n_items = 2 * SWA_KV_HEADS

    @pl.when(first)
    def _():
        kpad_ref[:, 0:win, :] = jnp.zeros((n_items, win, LANES), BF16)
        vpad_ref[:, 0:win, :] = jnp.zeros((n_items, win, LANES), BF16)

    low_half = lax.broadcasted_iota(jnp.int32, (SWA_PART, LANES), 1) < d

    def project(part):
        r0 = part * SWA_PART
        h = _rms(x_ref[0, r0:r0 + SWA_PART, :], g_ref[...]).astype(BF16)
        qkv = _dot(h, wqkv_ref[...]) + bqkv_ref[...]
        cos = cos_ref[r0:r0 + SWA_PART, :]
        sin = sin_ref[r0:r0 + SWA_PART, :]
        q = (_rope_apply(qkv[:, :SWA_Q], cos, sin) * (d ** -0.5)).astype(BF16)
        k = _rope_apply(qkv[:, SWA_Q:SWA_Q + SWA_KV], cos, sin).astype(BF16)
        v = qkv[:, SWA_Q + SWA_KV:].astype(BF16)
        dst_rows = slice(win + r0, win + r0 + SWA_PART)
        for src, dst in ((k, kpad_ref), (v, vpad_ref)):
            for pair in range(SWA_KV // LANES):
                col = src[:, pair * LANES:(pair + 1) * LANES]
                swapped = jnp.concatenate([col[:, d:], col[:, :d]], axis=1)
                zero = jnp.zeros_like(col)
                dst[4 * pair + 0, dst_rows, :] = jnp.where(low_half, col, zero)
                dst[4 * pair + 1, dst_rows, :] = jnp.where(low_half, zero, swapped)
                dst[4 * pair + 2, dst_rows, :] = jnp.where(low_half, swapped, zero)
                dst[4 * pair + 3, dst_rows, :] = jnp.where(low_half, zero, col)
        return q

    items = [(kv, par) for kv in range(SWA_KV_HEADS) for par in range(2)]

    def qk_stage(q, blk):
        lr = (blk * win) % SWA_PART
        out = {}
        for kv, par in items:
            c0 = kv * grp * d
            q2 = jnp.concatenate([q[lr:lr + win, c0:c0 + LANES], q[lr:lr + win, c0 + LANES:c0 + 2 * LANES]], axis=0)
            out[kv, par] = _dot_nt(q2, kpad_ref[2 * kv + par, blk * win:(blk + 2) * win, :])
        return out

    qi = lax.broadcasted_iota(jnp.int32, (2 * win, win), 0) % win
    kj = lax.broadcasted_iota(jnp.int32, (2 * win, win), 1)
    from_prev = kj > qi
    srow = lax.broadcasted_iota(jnp.int32, (2 * win, 1), 0)
    prev_bias = jnp.where(first, NEG_INF, 0.0)

    def softmax_pv_stage(scores, blk):
        probs = {}
        for kv, par in items:
            s = scores[kv, par]
            s_prev = s[:, :win] + prev_bias if blk == 0 else s[:, :win]
            folded = jnp.where(from_prev, s_prev, s[:, win:])
            sink = jnp.where(srow < win, sinks_ref[kv * grp + par], sinks_ref[kv * grp + 2 + par])
            mx = jnp.maximum(jnp.max(folded, axis=-1, keepdims=True), sink)
            e = jnp.exp(folded - mx)
            inv_den = 1.0 / (jnp.sum(e, axis=-1, keepdims=True) + jnp.exp(sink - mx))
            p = (e * inv_den).astype(BF16)
            zero = jnp.zeros_like(p)
            probs[kv, par] = jnp.concatenate([jnp.where(from_prev, p, zero), jnp.where(from_prev, zero, p)], axis=1)
        for kv in range(SWA_KV_HEADS):
            c0 = kv * grp * d
            out = (_dot(probs[kv, 0], vpad_ref[2 * kv, blk * win:(blk + 2) * win, :])
                   + _dot(probs[kv, 1], vpad_ref[2 * kv + 1, blk * win:(blk + 2) * win, :]))
            attn_ref[blk * win:(blk + 1) * win, c0:c0 + LANES] = out[:win].astype(BF16)
            attn_ref[blk * win:(blk + 1) * win, c0 + LANES:c0 + 2 * LANES] = out[win:].astype(BF16)

    n_parts = rows // SWA_PART
    blk_per_part = SWA_PART // win
    qs = {part: project(part) for part in range(min(SWA_LOOKAHEAD, n_parts))}
    scores = {blk: qk_stage(qs[0], blk) for blk in range(blk_per_part)}
    for part in range(n_parts):
        nxt = part + 1
        if part + SWA_LOOKAHEAD < n_parts:
            qs[part + SWA_LOOKAHEAD] = project(part + SWA_LOOKAHEAD)
        for blk in range(part * blk_per_part, nxt * blk_per_part):
            softmax_pv_stage(scores.pop(blk), blk)
        if nxt < n_parts:
            for blk in range(nxt * blk_per_part, (nxt + 1) * blk_per_part):
                scores[blk] = qk_stage(qs[nxt], blk)
        rs = slice(part * SWA_PART, nxt * SWA_PART)
        o_ref[0, rs, :] = x_ref[0, rs, :] + _dot(attn_ref[rs, :], wo_ref[...]) + bo_ref[...]

    kpad_ref[:, 0:win, :] = kpad_ref[:, rows:rows + win, :]
    vpad_ref[:, 0:win, :] = vpad_ref[:, rows:rows + win, :]


def _swa(x, norm_g, w_qkv16, b_qkv, sinks, w_o16, b_o):
    b, t, _ = x.shape
    assert t % SWA_ROWS == 0 and SWA_ROWS % SWA_PART == 0 and SWA_PART % SWA_WINDOW == 0
    cos_t, sin_t = _rope_tables(t)
    n_qkv = SWA_Q + 2 * SWA_KV
    tile = pl.BlockSpec((1, SWA_ROWS, D_MODEL), lambda bi, ti, *_: (bi, ti, 0))
    table = pl.BlockSpec((SWA_ROWS, LANES), lambda bi, ti, *_: (ti, 0))
    pad_shape = (2 * SWA_KV_HEADS, SWA_WINDOW + SWA_ROWS, LANES)
    return pl.pallas_call(
        _swa_body,
        grid_spec=pltpu.PrefetchScalarGridSpec(
            num_scalar_prefetch=1,
            grid=(b, t // SWA_ROWS),
            in_specs=[tile, _resident((1, D_MODEL)), _resident((D_MODEL, n_qkv)), _resident((1, n_qkv)),
                      table, table, _resident((SWA_Q, D_MODEL)), _resident((1, D_MODEL))],
            out_specs=tile,
            scratch_shapes=[pltpu.VMEM(pad_shape, BF16), pltpu.VMEM(pad_shape, BF16),
                            pltpu.VMEM((SWA_ROWS, SWA_Q), BF16)]),
        out_shape=jax.ShapeDtypeStruct(x.shape, F32),
        compiler_params=pltpu.CompilerParams(dimension_semantics=("arbitrary", "arbitrary"),
                                             vmem_limit_bytes=VMEM_LIMIT_BYTES),
        name="swa_mixer",
    )(sinks.astype(F32), x, norm_g, w_qkv16, b_qkv.reshape(1, n_qkv).astype(F32), cos_t, sin_t,
      w_o16, b_o.reshape(1, D_MODEL).astype(F32))


def kernel(x, ffn1_norm, ffn1_w_gate, ffn1_w_up, ffn1_w_down, mix_norm, ffn2_norm, ffn2_w_gate, ffn2_w_up,
           ffn2_w_down, hyb_w_in, gdn_conv_w, gdn_a_log, gdn_dt_bias, gdn_out_norm, sc_conv_w, hyb_w_out,
           swa_w_qkv, swa_b_qkv, swa_sinks, swa_w_o, swa_b_o, final_norm):
    b, t, d = x.shape
    depth = ffn1_norm.shape[0]
    final_g = final_norm.reshape(1, d).astype(F32)
    ffn1 = [w.astype(F32) for w in (ffn1_w_gate, ffn1_w_up, ffn1_w_down)]
    ffn2 = [w.astype(F32) for w in (ffn2_w_gate, ffn2_w_up, ffn2_w_down)]
    hyb_in16, hyb_out16 = hyb_w_in.astype(BF16), hyb_w_out.astype(BF16)
    swa_qkv16, swa_o16 = swa_w_qkv.astype(BF16), swa_w_o.astype(BF16)

    def ffn(xin, norm, ws, li, last):
        y = _ffn(xin.reshape(b * t, d), norm.reshape(1, d).astype(F32), ws[0], ws[1], ws[2], li, final_g, last)
        return y.reshape(b, t, d)

    for li in range(depth):
        x = ffn(x, ffn1_norm[li], ffn1, li, False)
        norm = mix_norm[li].reshape(1, d).astype(F32)
        j = li // 2
        if li % 2 == 0:
            x = _hybrid(x, norm, hyb_in16[j], gdn_conv_w[j], gdn_a_log[j], gdn_dt_bias[j], gdn_out_norm[j],
                        sc_conv_w[j], hyb_out16[j])
        else:
            x = _swa(x, norm, swa_qkv16[j], swa_b_qkv[j], swa_sinks[j], swa_o16[j], swa_b_o[j])
        x = ffn(x, ffn2_norm[li], ffn2, li, li == depth - 1)
    return x
```

```python
import functools

import jax
import jax.numpy as jnp
from jax import lax
from jax.experimental import pallas as pl
from jax.experimental.pallas import tpu as pltpu

F32 = jnp.float32
BF16 = jnp.bfloat16

D_MODEL = 1024
D_FF = 2816
GDN_HEADS = 4
GDN_HEAD_DIM = 128
GDN_CONV = 4
GDN_QKV = 3 * GDN_HEADS * GDN_HEAD_DIM
GDN_Z = GDN_HEADS * GDN_HEAD_DIM
SC_CHANNELS = 512
SC_CONV = 3
SWA_HEADS = 16
SWA_KV_HEADS = 4
SWA_HEAD_DIM = 64
SWA_WINDOW = 128
SWA_Q = SWA_HEADS * SWA_HEAD_DIM
SWA_KV = SWA_KV_HEADS * SWA_HEAD_DIM
ROPE_THETA = 10000.0
RMS_EPS = 1e-6
L2_EPS = 1e-6

LANES = 128
SUBLANES = 8
VMEM_LIMIT_BYTES = 56 * 1024 * 1024

FFN_ROWS = 1024
FFN_PARTS = 4
FFN_W_CHUNKS = 8
FFN_COLS = 256
HYB_ROWS = 512
HYB_OUT_COLS = 256
HYB_ZSC_COLS = 256
HYB_ZSC_EARLY = 8
HYB_ZSC_MID = 0
GDN_UNIT = 128
SWA_ROWS = 1024
SWA_PART = 256
SWA_LOOKAHEAD = 1
CONV_HALO = SUBLANES
ROPE_SPAN = 128

NEG_INF = float("-inf")


def _rms(x, g):
    ms = jnp.mean(x * x, axis=-1, keepdims=True)
    return x * lax.rsqrt(ms + RMS_EPS) * g


def _sigmoid(x):
    return 1.0 / (1.0 + jnp.exp(-x))


def _silu(x):
    return x * _sigmoid(x)


def _dot(a, b):
    return jnp.dot(a, b, preferred_element_type=F32)


def _dot_nt(a, b):
    return lax.dot_general(a, b, (((1,), (1,)), ((), ())), preferred_element_type=F32)


def _dot_tn(a, b):
    return lax.dot_general(a, b, (((0,), (0,)), ((), ())), preferred_element_type=F32)


def _resident(shape):
    nd = len(shape)
    return pl.BlockSpec(shape, lambda *_: (0,) * nd, pipeline_mode=pl.Buffered(1))


def _resident_layer(shape, layer):
    nd = len(shape)
    return pl.BlockSpec((pl.squeezed,) + tuple(shape), lambda *_: (layer,) + (0,) * nd,
                        pipeline_mode=pl.Buffered(1))


def _load_cast(src_hbm, dst_ref, stage_ref, sem_ref):
    chunk = stage_ref.shape[1]
    n_chunks = src_hbm.shape[0] // chunk
    assert n_chunks * chunk == src_hbm.shape[0]

    def copy(k, slot):
        rows = pl.ds(pl.multiple_of(k * chunk, chunk), chunk)
        return pltpu.make_async_copy(src_hbm.at[rows, :], stage_ref.at[slot], sem_ref.at[slot])

    copy(0, 0).start()

    def body(k, carry):
        slot = k % 2

        @pl.when(k + 1 < n_chunks)
        def _():
            copy(k + 1, 1 - slot).start()

        copy(k, slot).wait()
        dst_ref[pl.ds(pl.multiple_of(k * chunk, chunk), chunk), :] = stage_ref[slot].astype(BF16)
        return carry

    lax.fori_loop(0, n_chunks, body, 0)


def _ffn_body(x_ref, g_ref, wg_hbm, wu_hbm, wd_hbm, fg_ref, o_ref, act_ref, wg_ref, wu_ref, wd_ref,
              stage_in_ref, stage_out_ref, sem_ref, *, layer, final_norm):
    @pl.when(pl.program_id(0) == 0)
    def _():
        _load_cast(wg_hbm.at[layer], wg_ref, stage_in_ref, sem_ref)
        _load_cast(wu_hbm.at[layer], wu_ref, stage_in_ref, sem_ref)
        _load_cast(wd_hbm.at[layer], wd_ref, stage_out_ref, sem_ref)

    part = x_ref.shape[0] // FFN_PARTS
    halves = [slice(i * part, (i + 1) * part) for i in range(FFN_PARTS)]
    hs = [_rms(x_ref[rs, :], g_ref[...]).astype(BF16) for rs in halves]
    for c in range(D_FF // FFN_COLS):
        cols = slice(c * FFN_COLS, (c + 1) * FFN_COLS)
        for h, rs in zip(hs, halves):
            gate = _dot(h, wg_ref[:, cols])
            up = _dot(h, wu_ref[:, cols])
            act_ref[rs, cols] = (_silu(gate) * up).astype(BF16)
    for rs in halves:
        y = x_ref[rs, :] + 0.5 * _dot(act_ref[rs, :], wd_ref[...])
        if final_norm:
            y = _rms(y, fg_ref[...])
        o_ref[rs, :] = y


def _ffn(x2d, norm_g, wg, wu, wd, layer, final_g, final_norm):
    rows = x2d.shape[0]
    assert rows % FFN_ROWS == 0
    row_spec = pl.BlockSpec((FFN_ROWS, D_MODEL), lambda i: (i, 0))
    hbm = pl.BlockSpec(memory_space=pl.ANY)
    return pl.pallas_call(
        functools.partial(_ffn_body, layer=layer, final_norm=final_norm),
        grid=(rows // FFN_ROWS,),
        in_specs=[row_spec, _resident((1, D_MODEL)), hbm, hbm, hbm, _resident((1, D_MODEL))],
        out_specs=row_spec,
        out_shape=jax.ShapeDtypeStruct((rows, D_MODEL), F32),
        scratch_shapes=[pltpu.VMEM((FFN_ROWS, D_FF), BF16),
                        pltpu.VMEM((D_MODEL, D_FF), BF16), pltpu.VMEM((D_MODEL, D_FF), BF16),
                        pltpu.VMEM((D_FF, D_MODEL), BF16),
                        pltpu.VMEM((2, D_MODEL // FFN_W_CHUNKS, D_FF), F32),
                        pltpu.VMEM((2, D_FF // FFN_W_CHUNKS, D_MODEL), F32),
                        pltpu.SemaphoreType.DMA((2,))],
        compiler_params=pltpu.CompilerParams(dimension_semantics=("arbitrary",),
                                             vmem_limit_bytes=VMEM_LIMIT_BYTES),
        name="ffn_final" if final_norm else "ffn",
    )(x2d, norm_g, wg, wu, wd, final_g)


def _causal_conv(halo_ref, cur, w_ref, width, cs):
    rows = cur.shape[0]
    ext = jnp.concatenate([halo_ref[:, cs], cur], axis=0)
    halo_ref[:, cs] = cur[rows - CONV_HALO:, :]
    acc = ext * w_ref[0:1, cs]
    for j in range(1, width):
        acc = pltpu.roll(acc, 1, axis=0) + ext * w_ref[j:j + 1, cs]
    return acc[CONV_HALO:, :]


def _segment_cumsum(x, seg):
    pos = lax.broadcasted_iota(jnp.int32, x.shape, 0) % seg
    shift = 1
    while shift < seg:
        x = x + jnp.where(pos >= shift, pltpu.roll(x, shift, axis=0), 0.0)
        shift *= 2
    return x


def _hyb_body(x_ref, g_ref, wqkv_ref, wzsc_ref, wba_ref, convw_ref, alog_ref, dtb_ref, onorm_ref, scw_ref,
              wout_ref, o_ref, halo_ref, halo2_ref, state_ref):
    first = pl.program_id(1) == 0

    @pl.when(first)
    def _():
        state_ref[...] = jnp.zeros(state_ref.shape, F32)
        halo_ref[...] = jnp.zeros(halo_ref.shape, F32)
        halo2_ref[...] = jnp.zeros(halo2_ref.shape, F32)

    rows = x_ref.shape[1]
    hd = GDN_HEAD_DIM
    unit = GDN_UNIT
    x = x_ref[0]
    h = _rms(x, g_ref[...]).astype(BF16)
    p_ba = _dot(h, wba_ref[...])
    p_qkv = _dot(h, wqkv_ref[...])
    zsc = {}

    def project_zsc(piece):
        cs = slice(piece * HYB_ZSC_COLS, (piece + 1) * HYB_ZSC_COLS)
        zsc[piece] = _dot(h, wzsc_ref[:, cs])

    for piece in range(HYB_ZSC_EARLY):
        project_zsc(piece)

    def zsc_cols(rs, start, width):
        piece, off = divmod(start, HYB_ZSC_COLS)
        assert off + width <= HYB_ZSC_COLS
        return zsc[piece][rs, off:off + width]

    beta_all = _sigmoid(p_ba)
    sp_in = p_ba + dtb_ref[...]
    softplus = jnp.maximum(sp_in, 0.0) + jnp.log1p(jnp.exp(-jnp.abs(sp_in)))
    g_all = -jnp.exp(alog_ref[...]) * softplus
    gc_all = _segment_cumsum(g_all, unit)
    gc_rows = gc_all.T

    cols = []
    for cg in range(GDN_QKV // hd):
        cs = slice(cg * hd, (cg + 1) * hd)
        c = _silu(_causal_conv(halo_ref, p_qkv[:, cs], convw_ref, GDN_CONV, cs))
        if cg < 2 * GDN_HEADS:
            c = c * lax.rsqrt(jnp.sum(c * c, axis=-1, keepdims=True) + L2_EPS)
        if cg < GDN_HEADS:
            c = c * (hd ** -0.5)
        cols.append(c)

    row = lax.broadcasted_iota(jnp.int32, (unit, unit), 0)
    col = lax.broadcasted_iota(jnp.int32, (unit, unit), 1)
    lower_incl = row >= col
    lower_strict = row > col
    eye = jnp.where(row == col, 1.0, 0.0)

    n_units = rows // unit
    pairs = [(u, hh) for u in range(n_units) for hh in range(GDN_HEADS)]
    pre = {}
    for u, hh in pairs:
        rs = slice(u * unit, (u + 1) * unit)
        q = cols[hh][rs]
        k = cols[GDN_HEADS + hh][rs]
        v = cols[2 * GDN_HEADS + hh][rs]
        beta = beta_all[rs, hh:hh + 1]
        gc = gc_all[rs, GDN_HEADS + hh:GDN_HEADS + hh + 1]
        gc_row = gc_rows[GDN_HEADS + hh:GDN_HEADS + hh + 1, rs]
        g_last = gc[unit - 1:unit, :]
        e_gc = jnp.exp(gc)
        kb = k * beta
        pre[u, hh] = dict(
            decay=jnp.exp(jnp.where(lower_incl, gc - gc_row, NEG_INF)),
            kbq16=jnp.concatenate([kb, q], axis=0).astype(BF16),
            k16=k.astype(BF16),
            rhs16=jnp.concatenate([v * beta, kb * e_gc], axis=-1).astype(BF16),
            qg16=(q * e_gc).astype(BF16),
            kg16=(k * jnp.exp(g_last - gc)).astype(BF16),
            s_decay=jnp.exp(g_last))

    kq = {p: _dot_nt(pre[p]["kbq16"], pre[p]["k16"]) for p in pairs}
    neg_m, attn16, inv = {}, {}, {}
    for p in pairs:
        decay = pre[p]["decay"]
        n = jnp.where(lower_strict, -(kq[p][:unit] * decay), 0.0)
        neg_m[p] = n.astype(BF16)
        attn16[p] = (kq[p][unit:] * decay).astype(BF16)
        inv[p] = eye + n
    for piece in range(HYB_ZSC_EARLY, HYB_ZSC_EARLY + HYB_ZSC_MID):
        project_zsc(piece)
    span = 2
    while span < unit:
        resid = {p: ((eye - inv[p]) + _dot(neg_m[p], inv[p].astype(BF16))).astype(BF16) for p in pairs}
        inv = {p: inv[p] + _dot(inv[p].astype(BF16), resid[p]) for p in pairs}
        span *= 2

    sol, wq, v16, gdn_out, y_sc16 = {}, {}, {}, {}, {}
    all_rows = slice(0, rows)

    def short_conv():
        y_parts = []
        for cg in range(SC_CHANNELS // LANES):
            cs = slice(cg * LANES, (cg + 1) * LANES)
            sc_b = zsc_cols(all_rows, GDN_Z + cg * LANES, LANES)
            sc_ch = (zsc_cols(all_rows, GDN_Z + SC_CHANNELS + cg * LANES, LANES)
                     * zsc_cols(all_rows, GDN_Z + 2 * SC_CHANNELS + cg * LANES, LANES))
            y_parts.append(sc_b * _causal_conv(halo2_ref, sc_ch, scw_ref, SC_CONV, cs))
        y_sc16[0] = jnp.concatenate(y_parts, axis=-1).astype(BF16)

    def solve(u):
        for hh in range(GDN_HEADS):
            sol[u, hh] = _dot(inv[u, hh].astype(BF16), pre[u, hh]["rhs16"])

    def chain_read(u):
        for hh in range(GDN_HEADS):
            p = (u, hh)
            s16 = state_ref[hh].astype(BF16)
            wq[p] = _dot(jnp.concatenate([sol[p][:, hd:].astype(BF16), pre[p]["qg16"]], axis=0), s16)

    def chain_write(u):
        for hh in range(GDN_HEADS):
            p = (u, hh)
            v16[p] = (sol[p][:, :hd] - wq[p][:unit]).astype(BF16)
            state_ref[hh] = state_ref[hh] * pre[p]["s_decay"] + _dot_tn(pre[p]["kg16"], v16[p])

    def unit_output(u):
        rs = slice(u * unit, (u + 1) * unit)
        heads = []
        for hh in range(GDN_HEADS):
            p = (u, hh)
            o = wq[p][unit:] + _dot(attn16[p], v16[p])
            o = o * lax.rsqrt(jnp.mean(o * o, axis=-1, keepdims=True) + RMS_EPS)
            heads.append(o * onorm_ref[...] * _silu(zsc_cols(rs, hh * hd, hd)))
        gdn_out[u] = jnp.concatenate(heads, axis=-1).astype(BF16)

    def project_out(u0, col):
        rs = slice(u0 * unit, (u0 + 2) * unit)
        mix = jnp.concatenate([jnp.concatenate([gdn_out[u0], gdn_out[u0 + 1]], axis=0), y_sc16[0][rs]], axis=-1)
        cs = slice(col, col + HYB_OUT_COLS)
        o_ref[0, rs, cs] = x_ref[0, rs, cs] + _dot(mix, wout_ref[:, cs])

    assert n_units % 2 == 0
    late_pieces = [functools.partial(project_zsc, piece)
                   for piece in range(HYB_ZSC_EARLY + HYB_ZSC_MID, (GDN_Z + 3 * SC_CHANNELS) // HYB_ZSC_COLS)]
    fillers = []
    for u in range(1, n_units):
        fillers.append(functools.partial(solve, u))
        if late_pieces:
            fillers.append(late_pieces.pop(0))
    fillers += late_pieces + [short_conv]
    solve(0)
    for u in range(n_units):
        if (u, 0) not in sol:
            fillers = [f for f in fillers if not (f.func is solve and f.args == (u,))]
            solve(u)
        for step in (chain_read, chain_write):
            step(u)
            if fillers:
                fillers.pop(0)()
        fillers.append(functools.partial(unit_output, u))
        if u % 2 == 1:
            fillers += [functools.partial(project_out, u - 1, col) for col in range(0, D_MODEL, HYB_OUT_COLS)]
    for f in fillers:
        f()


def _hybrid(x, norm_g, w_in16, conv_w, a_log, dt_bias, out_norm, sc_conv_w, w_out16):
    b, t, _ = x.shape
    assert t % HYB_ROWS == 0 and HYB_ROWS % GDN_UNIT == 0
    n_qkvz = GDN_QKV + GDN_Z
    n_zsc = GDN_Z + 3 * SC_CHANNELS
    w_qkv = w_in16[:, :GDN_QKV]
    w_ba = jnp.pad(w_in16[:, n_qkvz:n_qkvz + 2 * GDN_HEADS], ((0, 0), (0, LANES - 2 * GDN_HEADS)))
    w_zsc = jnp.concatenate([w_in16[:, GDN_QKV:n_qkvz], w_in16[:, n_qkvz + 2 * GDN_HEADS:]], axis=1)
    lane_pad = (GDN_HEADS, LANES - 2 * GDN_HEADS)
    alog_row = jnp.pad(a_log.astype(F32), lane_pad).reshape(1, LANES)
    dtb_row = jnp.pad(dt_bias.astype(F32), lane_pad).reshape(1, LANES)
    tile = pl.BlockSpec((1, HYB_ROWS, D_MODEL), lambda bi, ti: (bi, ti, 0))
    return pl.pallas_call(
        _hyb_body,
        grid=(b, t // HYB_ROWS),
        in_specs=[tile, _resident((1, D_MODEL)), _resident((D_MODEL, GDN_QKV)), _resident((D_MODEL, n_zsc)),
                  _resident((D_MODEL, LANES)), _resident((GDN_CONV, GDN_QKV)), _resident((1, LANES)),
                  _resident((1, LANES)), _resident((1, GDN_HEAD_DIM)), _resident((SC_CONV, SC_CHANNELS)),
                  _resident((GDN_Z + SC_CHANNELS, D_MODEL))],
        out_specs=tile,
        out_shape=jax.ShapeDtypeStruct(x.shape, F32),
        scratch_shapes=[pltpu.VMEM((CONV_HALO, GDN_QKV), F32),
                        pltpu.VMEM((CONV_HALO, SC_CHANNELS), F32),
                        pltpu.VMEM((GDN_HEADS, GDN_HEAD_DIM, GDN_HEAD_DIM), F32)],
        compiler_params=pltpu.CompilerParams(dimension_semantics=("arbitrary", "arbitrary"),
                                             vmem_limit_bytes=VMEM_LIMIT_BYTES),
        name="hybrid_mixer",
    )(x, norm_g, w_qkv, w_zsc, w_ba, conv_w.astype(F32), alog_row, dtb_row,
      out_norm.reshape(1, GDN_HEAD_DIM).astype(F32), sc_conv_w.astype(F32), w_out16)


def _rope_body(inv_ref, cos_ref, sin_ref, cos_lo_ref, sin_lo_ref):
    rows = cos_ref.shape[0]
    half = SWA_HEAD_DIM // 2
    inv = inv_ref[...]
    lane = lax.broadcasted_iota(jnp.int32, (1, LANES), 1)
    sign = jnp.where(lane % SWA_HEAD_DIM < half, -1.0, 1.0)

    @pl.when(pl.program_id(0) == 0)
    def _():
        low = lax.broadcasted_iota(jnp.int32, (ROPE_SPAN, LANES), 0).astype(F32) * inv
        cos_lo_ref[...] = jnp.cos(low)
        sin_lo_ref[...] = jnp.sin(low)

    n_hi = rows // ROPE_SPAN
    a = lax.broadcasted_iota(jnp.int32, (n_hi, LANES), 0) + pl.program_id(0) * n_hi
    high = (a * ROPE_SPAN).astype(F32) * inv
    cos_hi, sin_hi = jnp.cos(high), jnp.sin(high)
    cos_lo, sin_lo = cos_lo_ref[...], sin_lo_ref[...]
    for i in range(n_hi):
        rs = slice(i * ROPE_SPAN, (i + 1) * ROPE_SPAN)
        c, s = cos_hi[i:i + 1, :], sin_hi[i:i + 1, :]
        cos_ref[rs, :] = c * cos_lo - s * sin_lo
        sin_ref[rs, :] = sign * (s * cos_lo + c * sin_lo)


def _rope_tables(t):
    half = SWA_HEAD_DIM // 2
    inv = ROPE_THETA ** (-jnp.arange(half, dtype=F32) / half)
    inv_row = jnp.tile(inv, LANES // half).reshape(1, LANES)
    rows = min(t, 1024)
    assert t % rows == 0 and rows % ROPE_SPAN == 0
    spec = pl.BlockSpec((rows, LANES), lambda i: (i, 0))
    return pl.pallas_call(
        _rope_body,
        grid=(t // rows,),
        in_specs=[pl.BlockSpec((1, LANES), lambda i: (0, 0))],
        out_specs=[spec, spec],
        out_shape=[jax.ShapeDtypeStruct((t, LANES), F32)] * 2,
        scratch_shapes=[pltpu.VMEM((ROPE_SPAN, LANES), F32)] * 2,
        compiler_params=pltpu.CompilerParams(dimension_semantics=("arbitrary",)),
        name="rope_tables",
    )(inv_row)


def _rope_apply(x, cos, sin_signed):
    half = SWA_HEAD_DIM // 2
    lane = lax.broadcasted_iota(jnp.int32, (x.shape[0], LANES), 1)
    first_half = lane % SWA_HEAD_DIM < half
    out = []
    for c in range(x.shape[1] // LANES):
        col = x[:, c * LANES:(c + 1) * LANES]
        swapped = jnp.where(first_half,
                            pltpu.roll(col, LANES - half, axis=1),
                            pltpu.roll(col, half, axis=1))
        out.append(col * cos + swapped * sin_signed)
    return jnp.concatenate(out, axis=1)


def _swa_body(sinks_ref, x_ref, g_ref, wqkv_ref, bqkv_ref, cos_ref, sin_ref, wo_ref, bo_ref, o_ref,
              kpad_ref, vpad_ref, attn_ref):
    first = pl.program_id(1) == 0
    rows = x_ref.shape[1]
    win = SWA_WINDOW
    d = SWA_HEAD_DIM
    grp = SWA_HEADS // SWA_KV_HEADS
    n_items = 2 * SWA_KV_HEADS

    @pl.when(first)
    def _():
        kpad_ref[:, 0:win, :] = jnp.zeros((n_items, win, LANES), BF16)
        vpad_ref[:, 0:win, :] = jnp.zeros((n_items, win, LANES), BF16)

    low_half = lax.broadcasted_iota(jnp.int32, (SWA_PART, LANES), 1) < d

    def project(part):
        r0 = part * SWA_PART
        h = _rms(x_ref[0, r0:r0 + SWA_PART, :], g_ref[...]).astype(BF16)
        qkv = _dot(h, wqkv_ref[...]) + bqkv_ref[...]
        cos = cos_ref[r0:r0 + SWA_PART, :]
        sin = sin_ref[r0:r0 + SWA_PART, :]
        q = (_rope_apply(qkv[:, :SWA_Q], cos, sin) * (d ** -0.5)).astype(BF16)
        k = _rope_apply(qkv[:, SWA_Q:SWA_Q + SWA_KV], cos, sin).astype(BF16)
        v = qkv[:, SWA_Q + SWA_KV:].astype(BF16)
        dst_rows = slice(win + r0, win + r0 + SWA_PART)
        for src, dst in ((k, kpad_ref), (v, vpad_ref)):
            for pair in range(SWA_KV // LANES):
                col = src[:, pair * LANES:(pair + 1) * LANES]
                swapped = jnp.concatenate([col[:, d:], col[:, :d]], axis=1)
                zero = jnp.zeros_like(col)
                dst[4 * pair + 0, dst_rows, :] = jnp.where(low_half, col, zero)
                dst[4 * pair + 1, dst_rows, :] = jnp.where(low_half, zero, swapped)
                dst[4 * pair + 2, dst_rows, :] = jnp.where(low_half, swapped, zero)
                dst[4 * pair + 3, dst_rows, :] = jnp.where(low_half, zero, col)
        return q

    items = [(kv, par) for kv in range(SWA_KV_HEADS) for par in range(2)]

    qi = lax.broadcasted_iota(jnp.int32, (2 * win, win), 0) % win
    kj = lax.broadcasted_iota(jnp.int32, (2 * win, win), 1)
    from_prev = kj > qi
    srow = lax.broadcasted_iota(jnp.int32, (2 * win, 1), 0)
    prev_bias = jnp.where(first, NEG_INF, 0.0)

    def qk_stage(q, blk):
        lr = (blk * win) % SWA_PART
        out = {}
        for kv, par in items:
            c0 = kv * grp * d
            q2 = jnp.concatenate([q[lr:lr + win, c0:c0 + LANES], q[lr:lr + win, c0 + LANES:c0 + 2 * LANES]], axis=0)
            s = _dot_nt(q2, kpad_ref[2 * kv + par, blk * win:(blk + 2) * win, :])
            s_prev = s[:, :win] + prev_bias if blk == 0 else s[:, :win]
            out[kv, par] = jnp.where(from_prev, s_prev, s[:, win:])
        return out

    def softmax_pv_stage(scores, blk):
        probs = {}
        for kv, par in items:
            folded = scores[kv, par]
            sink = jnp.where(srow < win, sinks_ref[kv * grp + par], sinks_ref[kv * grp + 2 + par])
            mx = jnp.maximum(jnp.max(folded, axis=-1, keepdims=True), sink)
            e = jnp.exp(folded - mx)
            inv_den = 1.0 / (jnp.sum(e, axis=-1, keepdims=True) + jnp.exp(sink - mx))
            p = (e * inv_den).astype(BF16)
            zero = jnp.zeros_like(p)
            probs[kv, par] = jnp.concatenate([jnp.where(from_prev, p, zero), jnp.where(from_prev, zero, p)], axis=1)
        for kv in range(SWA_KV_HEADS):
            c0 = kv * grp * d
            out = (_dot(probs[kv, 0], vpad_ref[2 * kv, blk * win:(blk + 2) * win, :])
                   + _dot(probs[kv, 1], vpad_ref[2 * kv + 1, blk * win:(blk + 2) * win, :]))
            attn_ref[blk * win:(blk + 1) * win, c0:c0 + LANES] = out[:win].astype(BF16)
            attn_ref[blk * win:(blk + 1) * win, c0 + LANES:c0 + 2 * LANES] = out[win:].astype(BF16)

    n_parts = rows // SWA_PART
    blk_per_part = SWA_PART // win
    qs = {part: project(part) for part in range(min(SWA_LOOKAHEAD, n_parts))}
    scores = {blk: qk_stage(qs[0], blk) for blk in range(blk_per_part)}
    for part in range(n_parts):
        nxt = part + 1
        if part + SWA_LOOKAHEAD < n_parts:
            qs[part + SWA_LOOKAHEAD] = project(part + SWA_LOOKAHEAD)
        for blk in range(part * blk_per_part, nxt * blk_per_part):
            softmax_pv_stage(scores.pop(blk), blk)
        if nxt < n_parts:
            for blk in range(nxt * blk_per_part, (nxt + 1) * blk_per_part):
                scores[blk] = qk_stage(qs[nxt], blk)
        rs = slice(part * SWA_PART, nxt * SWA_PART)
        o_ref[0, rs, :] = x_ref[0, rs, :] + _dot(attn_ref[rs, :], wo_ref[...]) + bo_ref[...]

    kpad_ref[:, 0:win, :] = kpad_ref[:, rows:rows + win, :]
    vpad_ref[:, 0:win, :] = vpad_ref[:, rows:rows + win, :]


def _swa(x, norm_g, w_qkv16, b_qkv, sinks, w_o16, b_o):
    b, t, _ = x.shape
    assert t % SWA_ROWS == 0 and SWA_ROWS % SWA_PART == 0 and SWA_PART % SWA_WINDOW == 0
    cos_t, sin_t = _rope_tables(t)
    n_qkv = SWA_Q + 2 * SWA_KV
    tile = pl.BlockSpec((1, SWA_ROWS, D_MODEL), lambda bi, ti, *_: (bi, ti, 0))
    table = pl.BlockSpec((SWA_ROWS, LANES), lambda bi, ti, *_: (ti, 0))
    pad_shape = (2 * SWA_KV_HEADS, SWA_WINDOW + SWA_ROWS, LANES)
    return pl.pallas_call(
        _swa_body,
        grid_spec=pltpu.PrefetchScalarGridSpec(
            num_scalar_prefetch=1,
            grid=(b, t // SWA_ROWS),
            in_specs=[tile, _resident((1, D_MODEL)), _resident((D_MODEL, n_qkv)), _resident((1, n_qkv)),
                      table, table, _resident((SWA_Q, D_MODEL)), _resident((1, D_MODEL))],
            out_specs=tile,
            scratch_shapes=[pltpu.VMEM(pad_shape, BF16), pltpu.VMEM(pad_shape, BF16),
                            pltpu.VMEM((SWA_ROWS, SWA_Q), BF16)]),
        out_shape=jax.ShapeDtypeStruct(x.shape, F32),
        compiler_params=pltpu.CompilerParams(dimension_semantics=("arbitrary", "arbitrary"),
                                             vmem_limit_bytes=VMEM_LIMIT_BYTES),
        name="swa_mixer",
    )(sinks.astype(F32), x, norm_g, w_qkv16, b_qkv.reshape(1, n_qkv).astype(F32), cos_t, sin_t,
      w_o16, b_o.reshape(1, D_MODEL).astype(F32))


def kernel(x, ffn1_norm, ffn1_w_gate, ffn1_w_up, ffn1_w_down, mix_norm, ffn2_norm, ffn2_w_gate, ffn2_w_up,
           ffn2_w_down, hyb_w_in, gdn_conv_w, gdn_a_log, gdn_dt_bias, gdn_out_norm, sc_conv_w, hyb_w_out,
           swa_w_qkv, swa_b_qkv, swa_sinks, swa_w_o, swa_b_o, final_norm):
    b, t, d = x.shape
    depth = ffn1_norm.shape[0]
    final_g = final_norm.reshape(1, d).astype(F32)
    ffn1 = [w.astype(F32) for w in (ffn1_w_gate, ffn1_w_up, ffn1_w_down)]
    ffn2 = [w.astype(F32) for w in (ffn2_w_gate, ffn2_w_up, ffn2_w_down)]
    hyb_in16, hyb_out16 = hyb_w_in.astype(BF16), hyb_w_out.astype(BF16)
    swa_qkv16, swa_o16 = swa_w_qkv.astype(BF16), swa_w_o.astype(BF16)

    def ffn(xin, norm, ws, li, last):
        y = _ffn(xin.reshape(b * t, d), norm.reshape(1, d).astype(F32), ws[0], ws[1], ws[2], li, final_g, last)
        return y.reshape(b, t, d)

    for li in range(depth):
        x = ffn(x, ffn1_norm[li], ffn1, li, False)
        norm = mix_norm[li].reshape(1, d).astype(F32)
        j = li // 2
        if li % 2 == 0:
            x = _hybrid(x, norm, hyb_in16[j], gdn_conv_w[j], gdn_a_log[j], gdn_dt_bias[j], gdn_out_norm[j],
                        sc_conv_w[j], hyb_out16[j])
        else:
            x = _swa(x, norm, swa_qkv16[j], swa_b_qkv[j], swa_sinks[j], swa_o16[j], swa_b_o[j])
        x = ffn(x, ffn2_norm[li], ffn2, li, li == depth - 1)
    return x
```

```python
import functools

import jax
import jax.numpy as jnp
from jax import lax
from jax.experimental import pallas as pl
from jax.experimental.pallas import tpu as pltpu

F32 = jnp.float32
BF16 = jnp.bfloat16

D_MODEL = 1024
D_FF = 2816
GDN_HEADS = 4
GDN_HEAD_DIM = 128
GDN_CONV = 4
GDN_QKV = 3 * GDN_HEADS * GDN_HEAD_DIM
GDN_Z = GDN_HEADS * GDN_HEAD_DIM
SC_CHANNELS = 512
SC_CONV = 3
SWA_HEADS = 16
SWA_KV_HEADS = 4
SWA_HEAD_DIM = 64
SWA_WINDOW = 128
SWA_Q = SWA_HEADS * SWA_HEAD_DIM
SWA_KV = SWA_KV_HEADS * SWA_HEAD_DIM
ROPE_THETA = 10000.0
RMS_EPS = 1e-6
L2_EPS = 1e-6

LANES = 128
SUBLANES = 8
VMEM_LIMIT_BYTES = 56 * 1024 * 1024

FFN_ROWS = 1024
FFN_PARTS = 4
FFN_W_CHUNKS = 8
FFN_COLS = 256
HYB_ROWS = 512
HYB_OUT_COLS = 256
HYB_ZSC_COLS = 256
HYB_ZSC_EARLY = 8
HYB_ZSC_MID = 0
GDN_UNIT = 128
SWA_ROWS = 1024
SWA_PART = 256
SWA_LOOKAHEAD = 1
CONV_HALO = SUBLANES
ROPE_SPAN = 128

NEG_INF = float("-inf")


def _rms(x, g):
    ms = jnp.mean(x * x, axis=-1, keepdims=True)
    return x * lax.rsqrt(ms + RMS_EPS) * g


def _sigmoid(x):
    return 1.0 / (1.0 + jnp.exp(-x))


def _silu(x):
    return x * _sigmoid(x)


def _dot(a, b):
    return jnp.dot(a, b, preferred_element_type=F32)


def _dot_nt(a, b):
    return lax.dot_general(a, b, (((1,), (1,)), ((), ())), preferred_element_type=F32)


def _dot_tn(a, b):
    return lax.dot_general(a, b, (((0,), (0,)), ((), ())), preferred_element_type=F32)


def _resident(shape):
    nd = len(shape)
    return pl.BlockSpec(shape, lambda *_: (0,) * nd, pipeline_mode=pl.Buffered(1))


def _resident_layer(shape, layer):
    nd = len(shape)
    return pl.BlockSpec((pl.squeezed,) + tuple(shape), lambda *_: (layer,) + (0,) * nd,
                        pipeline_mode=pl.Buffered(1))


def _weight_chunk_copy(src_hbm, stage_ref, sem_ref, k, slot):
    chunk = stage_ref.shape[1]
    rows = pl.ds(pl.multiple_of(k * chunk, chunk), chunk)
    return pltpu.make_async_copy(src_hbm.at[rows, :], stage_ref.at[slot], sem_ref.at[slot])


def _load_cast(src_hbm, dst_ref, stage_ref, sem_ref):
    chunk = stage_ref.shape[1]
    n_chunks = src_hbm.shape[0] // chunk
    assert n_chunks * chunk == src_hbm.shape[0]
    copy = functools.partial(_weight_chunk_copy, src_hbm, stage_ref, sem_ref)

    def body(k, carry):
        slot = k % 2

        @pl.when(k + 1 < n_chunks)
        def _():
            copy(k + 1, 1 - slot).start()

        copy(k, slot).wait()
        dst_ref[pl.ds(pl.multiple_of(k * chunk, chunk), chunk), :] = stage_ref[slot].astype(BF16)
        return carry

    lax.fori_loop(0, n_chunks, body, 0)


def _ffn_body(x_ref, g_ref, wg_hbm, wu_hbm, wd_hbm, fg_ref, o_ref, act_ref, wg_ref, wu_ref, wd_ref,
              stage_g_ref, stage_u_ref, stage_d_ref, sem_ref, *, layer, final_norm):
    @pl.when(pl.program_id(0) == 0)
    def _():
        streams = ((wg_hbm.at[layer], wg_ref, stage_g_ref, sem_ref.at[0:2]),
                   (wu_hbm.at[layer], wu_ref, stage_u_ref, sem_ref.at[2:4]),
                   (wd_hbm.at[layer], wd_ref, stage_d_ref, sem_ref.at[4:6]))
        for src, _, stage, sems in streams:
            _weight_chunk_copy(src, stage, sems, 0, 0).start()
        for src, dst, stage, sems in streams:
            _load_cast(src, dst, stage, sems)

    part = x_ref.shape[0] // FFN_PARTS
    halves = [slice(i * part, (i + 1) * part) for i in range(FFN_PARTS)]
    hs = [_rms(x_ref[rs, :], g_ref[...]).astype(BF16) for rs in halves]
    for c in range(D_FF // FFN_COLS):
        cols = slice(c * FFN_COLS, (c + 1) * FFN_COLS)
        for h, rs in zip(hs, halves):
            gate = _dot(h, wg_ref[:, cols])
            up = _dot(h, wu_ref[:, cols])
            act_ref[rs, cols] = (_silu(gate) * up).astype(BF16)
    for rs in halves:
        y = x_ref[rs, :] + 0.5 * _dot(act_ref[rs, :], wd_ref[...])
        if final_norm:
            y = _rms(y, fg_ref[...])
        o_ref[rs, :] = y


def _ffn(x2d, norm_g, wg, wu, wd, layer, final_g, final_norm):
    rows = x2d.shape[0]
    assert rows % FFN_ROWS == 0
    row_spec = pl.BlockSpec((FFN_ROWS, D_MODEL), lambda i: (i, 0))
    hbm = pl.BlockSpec(memory_space=pl.ANY)
    return pl.pallas_call(
        functools.partial(_ffn_body, layer=layer, final_norm=final_norm),
        grid=(rows // FFN_ROWS,),
        in_specs=[row_spec, _resident((1, D_MODEL)), hbm, hbm, hbm, _resident((1, D_MODEL))],
        out_specs=row_spec,
        out_shape=jax.ShapeDtypeStruct((rows, D_MODEL), F32),
        scratch_shapes=[pltpu.VMEM((FFN_ROWS, D_FF), BF16),
                        pltpu.VMEM((D_MODEL, D_FF), BF16), pltpu.VMEM((D_MODEL, D_FF), BF16),
                        pltpu.VMEM((D_FF, D_MODEL), BF16),
                        pltpu.VMEM((2, D_MODEL // FFN_W_CHUNKS, D_FF), F32),
                        pltpu.VMEM((2, D_MODEL // FFN_W_CHUNKS, D_FF), F32),
                        pltpu.VMEM((2, D_FF // FFN_W_CHUNKS, D_MODEL), F32),
                        pltpu.SemaphoreType.DMA((6,))],
        compiler_params=pltpu.CompilerParams(dimension_semantics=("arbitrary",),
                                             vmem_limit_bytes=VMEM_LIMIT_BYTES),
        name="ffn_final" if final_norm else "ffn",
    )(x2d, norm_g, wg, wu, wd, final_g)


def _causal_conv(halo_ref, cur, w_ref, width, cs):
    rows = cur.shape[0]
    ext = jnp.concatenate([halo_ref[:, cs], cur], axis=0)
    halo_ref[:, cs] = cur[rows - CONV_HALO:, :]
    acc = ext * w_ref[0:1, cs]
    for j in range(1, width):
        acc = pltpu.roll(acc, 1, axis=0) + ext * w_ref[j:j + 1, cs]
    return acc[CONV_HALO:, :]


def _segment_cumsum(x, seg):
    pos = lax.broadcasted_iota(jnp.int32, x.shape, 0) % seg
    shift = 1
    while shift < seg:
        x = x + jnp.where(pos >= shift, pltpu.roll(x, shift, axis=0), 0.0)
        shift *= 2
    return x


def _hyb_body(x_ref, g_ref, wqkv_ref, wzsc_ref, wba_ref, convw_ref, alog_ref, dtb_ref, onorm_ref, scw_ref,
              wout_ref, o_ref, halo_ref, halo2_ref, state_ref):
    first = pl.program_id(1) == 0

    @pl.when(first)
    def _():
        state_ref[...] = jnp.zeros(state_ref.shape, F32)
        halo_ref[...] = jnp.zeros(halo_ref.shape, F32)
        halo2_ref[...] = jnp.zeros(halo2_ref.shape, F32)

    rows = x_ref.shape[1]
    hd = GDN_HEAD_DIM
    unit = GDN_UNIT
    x = x_ref[0]
    h = _rms(x, g_ref[...]).astype(BF16)
    p_ba = _dot(h, wba_ref[...])
    p_qkv = _dot(h, wqkv_ref[...])
    zsc = {}

    def project_zsc(piece):
        cs = slice(piece * HYB_ZSC_COLS, (piece + 1) * HYB_ZSC_COLS)
        zsc[piece] = _dot(h, wzsc_ref[:, cs])

    for piece in range(HYB_ZSC_EARLY):
        project_zsc(piece)

    def zsc_cols(rs, start, width):
        piece, off = divmod(start, HYB_ZSC_COLS)
        assert off + width <= HYB_ZSC_COLS
        return zsc[piece][rs, off:off + width]

    beta_all = _sigmoid(p_ba)
    sp_in = p_ba + dtb_ref[...]
    softplus = jnp.maximum(sp_in, 0.0) + jnp.log1p(jnp.exp(-jnp.abs(sp_in)))
    g_all = -jnp.exp(alog_ref[...]) * softplus
    gc_all = _segment_cumsum(g_all, unit)
    gc_rows = gc_all.T

    cols = []
    for cg in range(GDN_QKV // hd):
        cs = slice(cg * hd, (cg + 1) * hd)
        c = _silu(_causal_conv(halo_ref, p_qkv[:, cs], convw_ref, GDN_CONV, cs))
        if cg < 2 * GDN_HEADS:
            c = c * lax.rsqrt(jnp.sum(c * c, axis=-1, keepdims=True) + L2_EPS)
        if cg < GDN_HEADS:
            c = c * (hd ** -0.5)
        cols.append(c)

    row = lax.broadcasted_iota(jnp.int32, (unit, unit), 0)
    col = lax.broadcasted_iota(jnp.int32, (unit, unit), 1)
    lower_incl = row >= col
    lower_strict = row > col
    eye = jnp.where(row == col, 1.0, 0.0)

    n_units = rows // unit
    pairs = [(u, hh) for u in range(n_units) for hh in range(GDN_HEADS)]
    pre = {}
    for u, hh in pairs:
        rs = slice(u * unit, (u + 1) * unit)
        q = cols[hh][rs]
        k = cols[GDN_HEADS + hh][rs]
        v = cols[2 * GDN_HEADS + hh][rs]
        beta = beta_all[rs, hh:hh + 1]
        gc = gc_all[rs, GDN_HEADS + hh:GDN_HEADS + hh + 1]
        gc_row = gc_rows[GDN_HEADS + hh:GDN_HEADS + hh + 1, rs]
        g_last = gc[unit - 1:unit, :]
        e_gc = jnp.exp(gc)
        kb = k * beta
        pre[u, hh] = dict(
            decay=jnp.exp(jnp.where(lower_incl, gc - gc_row, NEG_INF)),
            kbq16=jnp.concatenate([kb, q], axis=0).astype(BF16),
            k16=k.astype(BF16),
            rhs16=jnp.concatenate([v * beta, kb * e_gc], axis=-1).astype(BF16),
            qg16=(q * e_gc).astype(BF16),
            kg16=(k * jnp.exp(g_last - gc)).astype(BF16),
            s_decay=jnp.exp(g_last))

    kq = {p: _dot_nt(pre[p]["kbq16"], pre[p]["k16"]) for p in pairs}
    neg_m, attn16, inv = {}, {}, {}
    for p in pairs:
        decay = pre[p]["decay"]
        n = jnp.where(lower_strict, -(kq[p][:unit] * decay), 0.0)
        neg_m[p] = n.astype(BF16)
        attn16[p] = (kq[p][unit:] * decay).astype(BF16)
        inv[p] = eye + n
    for piece in range(HYB_ZSC_EARLY, HYB_ZSC_EARLY + HYB_ZSC_MID):
        project_zsc(piece)
    span = 2
    while span < unit:
        resid = {p: ((eye - inv[p]) + _dot(neg_m[p], inv[p].astype(BF16))).astype(BF16) for p in pairs}
        inv = {p: inv[p] + _dot(inv[p].astype(BF16), resid[p]) for p in pairs}
        span *= 2

    sol, wq, v16, gdn_out, y_sc16 = {}, {}, {}, {}, {}
    all_rows = slice(0, rows)

    def short_conv():
        y_parts = []
        for cg in range(SC_CHANNELS // LANES):
            cs = slice(cg * LANES, (cg + 1) * LANES)
            sc_b = zsc_cols(all_rows, GDN_Z + cg * LANES, LANES)
            sc_ch = (zsc_cols(all_rows, GDN_Z + SC_CHANNELS + cg * LANES, LANES)
                     * zsc_cols(all_rows, GDN_Z + 2 * SC_CHANNELS + cg * LANES, LANES))
            y_parts.append(sc_b * _causal_conv(halo2_ref, sc_ch, scw_ref, SC_CONV, cs))
        y_sc16[0] = jnp.concatenate(y_parts, axis=-1).astype(BF16)

    def solve(u):
        for hh in range(GDN_HEADS):
            sol[u, hh] = _dot(inv[u, hh].astype(BF16), pre[u, hh]["rhs16"])

    def chain_read(u):
        for hh in range(GDN_HEADS):
            p = (u, hh)
            s16 = state_ref[hh].astype(BF16)
            wq[p] = _dot(jnp.concatenate([sol[p][:, hd:].astype(BF16), pre[p]["qg16"]], axis=0), s16)

    def chain_write(u):
        for hh in range(GDN_HEADS):
            p = (u, hh)
            v16[p] = (sol[p][:, :hd] - wq[p][:unit]).astype(BF16)
            state_ref[hh] = state_ref[hh] * pre[p]["s_decay"] + _dot_tn(pre[p]["kg16"], v16[p])

    def unit_output(u):
        rs = slice(u * unit, (u + 1) * unit)
        heads = []
        for hh in range(GDN_HEADS):
            p = (u, hh)
            o = wq[p][unit:] + _dot(attn16[p], v16[p])
            o = o * lax.rsqrt(jnp.mean(o * o, axis=-1, keepdims=True) + RMS_EPS)
            heads.append(o * onorm_ref[...] * _silu(zsc_cols(rs, hh * hd, hd)))
        gdn_out[u] = jnp.concatenate(heads, axis=-1).astype(BF16)

    def project_out(u0, col):
        rs = slice(u0 * unit, (u0 + 2) * unit)
        mix = jnp.concatenate([jnp.concatenate([gdn_out[u0], gdn_out[u0 + 1]], axis=0), y_sc16[0][rs]], axis=-1)
        cs = slice(col, col + HYB_OUT_COLS)
        o_ref[0, rs, cs] = x_ref[0, rs, cs] + _dot(mix, wout_ref[:, cs])

    assert n_units % 2 == 0
    late_pieces = [functools.partial(project_zsc, piece)
                   for piece in range(HYB_ZSC_EARLY + HYB_ZSC_MID, (GDN_Z + 3 * SC_CHANNELS) // HYB_ZSC_COLS)]
    fillers = []
    for u in range(1, n_units):
        fillers.append(functools.partial(solve, u))
        if late_pieces:
            fillers.append(late_pieces.pop(0))
    fillers += late_pieces + [short_conv]
    solve(0)
    for u in range(n_units):
        if (u, 0) not in sol:
            fillers = [f for f in fillers if not (f.func is solve and f.args == (u,))]
            solve(u)
        for step in (chain_read, chain_write):
            step(u)
            if fillers:
                fillers.pop(0)()
        fillers.append(functools.partial(unit_output, u))
        if u % 2 == 1:
            fillers += [functools.partial(project_out, u - 1, col) for col in range(0, D_MODEL, HYB_OUT_COLS)]
    for f in fillers:
        f()


def _hybrid(x, norm_g, w_in16, conv_w, a_log, dt_bias, out_norm, sc_conv_w, w_out16):
    b, t, _ = x.shape
    assert t % HYB_ROWS == 0 and HYB_ROWS % GDN_UNIT == 0
    n_qkvz = GDN_QKV + GDN_Z
    n_zsc = GDN_Z + 3 * SC_CHANNELS
    w_qkv = w_in16[:, :GDN_QKV]
    w_ba = jnp.pad(w_in16[:, n_qkvz:n_qkvz + 2 * GDN_HEADS], ((0, 0), (0, LANES - 2 * GDN_HEADS)))
    w_zsc = jnp.concatenate([w_in16[:, GDN_QKV:n_qkvz], w_in16[:, n_qkvz + 2 * GDN_HEADS:]], axis=1)
    lane_pad = (GDN_HEADS, LANES - 2 * GDN_HEADS)
    alog_row = jnp.pad(a_log.astype(F32), lane_pad).reshape(1, LANES)
    dtb_row = jnp.pad(dt_bias.astype(F32), lane_pad).reshape(1, LANES)
    tile = pl.BlockSpec((1, HYB_ROWS, D_MODEL), lambda bi, ti: (bi, ti, 0))
    return pl.pallas_call(
        _hyb_body,
        grid=(b, t // HYB_ROWS),
        in_specs=[tile, _resident((1, D_MODEL)), _resident((D_MODEL, GDN_QKV)), _resident((D_MODEL, n_zsc)),
                  _resident((D_MODEL, LANES)), _resident((GDN_CONV, GDN_QKV)), _resident((1, LANES)),
                  _resident((1, LANES)), _resident((1, GDN_HEAD_DIM)), _resident((SC_CONV, SC_CHANNELS)),
                  _resident((GDN_Z + SC_CHANNELS, D_MODEL))],
        out_specs=tile,
        out_shape=jax.ShapeDtypeStruct(x.shape, F32),
        scratch_shapes=[pltpu.VMEM((CONV_HALO, GDN_QKV), F32),
                        pltpu.VMEM((CONV_HALO, SC_CHANNELS), F32),
                        pltpu.VMEM((GDN_HEADS, GDN_HEAD_DIM, GDN_HEAD_DIM), F32)],
        compiler_params=pltpu.CompilerParams(dimension_semantics=("arbitrary", "arbitrary"),
                                             vmem_limit_bytes=VMEM_LIMIT_BYTES),
        name="hybrid_mixer",
    )(x, norm_g, w_qkv, w_zsc, w_ba, conv_w.astype(F32), alog_row, dtb_row,
      out_norm.reshape(1, GDN_HEAD_DIM).astype(F32), sc_conv_w.astype(F32), w_out16)


def _rope_body(inv_ref, cos_ref, sin_ref, cos_lo_ref, sin_lo_ref):
    rows = cos_ref.shape[0]
    half = SWA_HEAD_DIM // 2
    inv = inv_ref[...]
    lane = lax.broadcasted_iota(jnp.int32, (1, LANES), 1)
    sign = jnp.where(lane % SWA_HEAD_DIM < half, -1.0, 1.0)

    @pl.when(pl.program_id(0) == 0)
    def _():
        low = lax.broadcasted_iota(jnp.int32, (ROPE_SPAN, LANES), 0).astype(F32) * inv
        cos_lo_ref[...] = jnp.cos(low)
        sin_lo_ref[...] = jnp.sin(low)

    n_hi = rows // ROPE_SPAN
    a = lax.broadcasted_iota(jnp.int32, (n_hi, LANES), 0) + pl.program_id(0) * n_hi
    high = (a * ROPE_SPAN).astype(F32) * inv
    cos_hi, sin_hi = jnp.cos(high), jnp.sin(high)
    cos_lo, sin_lo = cos_lo_ref[...], sin_lo_ref[...]
    for i in range(n_hi):
        rs = slice(i * ROPE_SPAN, (i + 1) * ROPE_SPAN)
        c, s = cos_hi[i:i + 1, :], sin_hi[i:i + 1, :]
        cos_ref[rs, :] = c * cos_lo - s * sin_lo
        sin_ref[rs, :] = sign * (s * cos_lo + c * sin_lo)


def _rope_tables(t):
    half = SWA_HEAD_DIM // 2
    inv = ROPE_THETA ** (-jnp.arange(half, dtype=F32) / half)
    inv_row = jnp.tile(inv, LANES // half).reshape(1, LANES)
    rows = min(t, 1024)
    assert t % rows == 0 and rows % ROPE_SPAN == 0
    spec = pl.BlockSpec((rows, LANES), lambda i: (i, 0))
    return pl.pallas_call(
        _rope_body,
        grid=(t // rows,),
        in_specs=[pl.BlockSpec((1, LANES), lambda i: (0, 0))],
        out_specs=[spec, spec],
        out_shape=[jax.ShapeDtypeStruct((t, LANES), F32)] * 2,
        scratch_shapes=[pltpu.VMEM((ROPE_SPAN, LANES), F32)] * 2,
        compiler_params=pltpu.CompilerParams(dimension_semantics=("arbitrary",)),
        name="rope_tables",
    )(inv_row)


def _rope_apply(x, cos, sin_signed):
    half = SWA_HEAD_DIM // 2
    lane = lax.broadcasted_iota(jnp.int32, (x.shape[0], LANES), 1)
    first_half = lane % SWA_HEAD_DIM < half
    out = []
    for c in range(x.shape[1] // LANES):
        col = x[:, c * LANES:(c + 1) * LANES]
        swapped = jnp.where(first_half,
                            pltpu.roll(col, LANES - half, axis=1),
                            pltpu.roll(col, half, axis=1))
        out.append(col * cos + swapped * sin_signed)
    return jnp.concatenate(out, axis=1)


def _swa_body(sinks_ref, x_ref, g_ref, wqkv_ref, bqkv_ref, cos_ref, sin_ref, wo_ref, bo_ref, o_ref,
              kpad_ref, vpad_ref, attn_ref):
    first = pl.program_id(1) == 0
    rows = x_ref.shape[1]
    win = SWA_WINDOW
    d = SWA_HEAD_DIM
    grp = SWA_HEADS // SWA_KV_HEADS
    n_items = 2 * SWA_KV_HEADS

    @pl.when(first)
    def _():
        kpad_ref[:, 0:win, :] = jnp.zeros((n_items, win, LANES), BF16)
        vpad_ref[:, 0:win, :] = jnp.zeros((n_items, win, LANES), BF16)

    low_half = lax.broadcasted_iota(jnp.int32, (SWA_PART, LANES), 1) < d

    def project(part):
        r0 = part * SWA_PART
        h = _rms(x_ref[0, r0:r0 + SWA_PART, :], g_ref[...]).astype(BF16)
        qkv = _dot(h, wqkv_ref[...]) + bqkv_ref[...]
        cos = cos_ref[r0:r0 + SWA_PART, :]
        sin = sin_ref[r0:r0 + SWA_PART, :]
        q = (_rope_apply(qkv[:, :SWA_Q], cos, sin) * (d ** -0.5)).astype(BF16)
        k = _rope_apply(qkv[:, SWA_Q:SWA_Q + SWA_KV], cos, sin).astype(BF16)
        v = qkv[:, SWA_Q + SWA_KV:].astype(BF16)
        dst_rows = slice(win + r0, win + r0 + SWA_PART)
        for src, dst in ((k, kpad_ref), (v, vpad_ref)):
            for pair in range(SWA_KV // LANES):
                col = src[:, pair * LANES:(pair + 1) * LANES]
                swapped = jnp.concatenate([col[:, d:], col[:, :d]], axis=1)
                zero = jnp.zeros_like(col)
                dst[4 * pair + 0, dst_rows, :] = jnp.where(low_half, col, zero)
                dst[4 * pair + 1, dst_rows, :] = jnp.where(low_half, zero, swapped)
                dst[4 * pair + 2, dst_rows, :] = jnp.where(low_half, swapped, zero)
                dst[4 * pair + 3, dst_rows, :] = jnp.where(low_half, zero, col)
        return q

    items = [(kv, par) for kv in range(SWA_KV_HEADS) for par in range(2)]

    qi = lax.broadcasted_iota(jnp.int32, (2 * win, win), 0) % win
    kj = lax.broadcasted_iota(jnp.int32, (2 * win, win), 1)
    from_prev = kj > qi
    srow = lax.broadcasted_iota(jnp.int32, (2 * win, 1), 0)
    prev_bias = jnp.where(first, NEG_INF, 0.0)

    def qk_stage(q, blk):
        lr = (blk * win) % SWA_PART
        out = {}
        for kv, par in items:
            c0 = kv * grp * d
            q2 = jnp.concatenate([q[lr:lr + win, c0:c0 + LANES], q[lr:lr + win, c0 + LANES:c0 + 2 * LANES]], axis=0)
            s = _dot_nt(q2, kpad_ref[2 * kv + par, blk * win:(blk + 2) * win, :])
            s_prev = s[:, :win] + prev_bias if blk == 0 else s[:, :win]
            out[kv, par] = jnp.where(from_prev, s_prev, s[:, win:])
        return out

    def softmax_pv_stage(scores, blk):
        probs = {}
        for kv, par in items:
            folded = scores[kv, par]
            sink = jnp.where(srow < win, sinks_ref[kv * grp + par], sinks_ref[kv * grp + 2 + par])
            mx = jnp.maximum(jnp.max(folded, axis=-1, keepdims=True), sink)
            e = jnp.exp(folded - mx)
            inv_den = 1.0 / (jnp.sum(e, axis=-1, keepdims=True) + jnp.exp(sink - mx))
            p = (e * inv_den).astype(BF16)
            zero = jnp.zeros_like(p)
            probs[kv, par] = jnp.concatenate([jnp.where(from_prev, p, zero), jnp.where(from_prev, zero, p)], axis=1)
        for kv in range(SWA_KV_HEADS):
            c0 = kv * grp * d
            out = (_dot(probs[kv, 0], vpad_ref[2 * kv, blk * win:(blk + 2) * win, :])
                   + _dot(probs[kv, 1], vpad_ref[2 * kv + 1, blk * win:(blk + 2) * win, :]))
            attn_ref[blk * win:(blk + 1) * win, c0:c0 + LANES] = out[:win].astype(BF16)
            attn_ref[blk * win:(blk + 1) * win, c0 + LANES:c0 + 2 * LANES] = out[win:].astype(BF16)

    n_parts = rows // SWA_PART
    blk_per_part = SWA_PART // win
    qs = {part: project(part) for part in range(min(SWA_LOOKAHEAD, n_parts))}
    scores = {blk: qk_stage(qs[0], blk) for blk in range(blk_per_part)}
    for part in range(n_parts):
        nxt = part + 1
        if part + SWA_LOOKAHEAD < n_parts:
            qs[part + SWA_LOOKAHEAD] = project(part + SWA_LOOKAHEAD)
        for blk in range(part * blk_per_part, nxt * blk_per_part):
            softmax_pv_stage(scores.pop(blk), blk)
        if nxt < n_parts:
            for blk in range(nxt * blk_per_part, (nxt + 1) * blk_per_part):
                scores[blk] = qk_stage(qs[nxt], blk)
        rs = slice(part * SWA_PART, nxt * SWA_PART)
        o_ref[0, rs, :] = x_ref[0, rs, :] + _dot(attn_ref[rs, :], wo_ref[...]) + bo_ref[...]

    kpad_ref[:, 0:win, :] = kpad_ref[:, rows:rows + win, :]
    vpad_ref[:, 0:win, :] = vpad_ref[:, rows:rows + win, :]


def _swa(x, norm_g, w_qkv16, b_qkv, sinks, w_o16, b_o):
    b, t, _ = x.shape
    assert t % SWA_ROWS == 0 and SWA_ROWS % SWA_PART == 0 and SWA_PART % SWA_WINDOW == 0
    cos_t, sin_t = _rope_tables(t)
    n_qkv = SWA_Q + 2 * SWA_KV
    tile = pl.BlockSpec((1, SWA_ROWS, D_MODEL), lambda bi, ti, *_: (bi, ti, 0))
    table = pl.BlockSpec((SWA_ROWS, LANES), lambda bi, ti, *_: (ti, 0))
    pad_shape = (2 * SWA_KV_HEADS, SWA_WINDOW + SWA_ROWS, LANES)
    return pl.pallas_call(
        _swa_body,
        grid_spec=pltpu.PrefetchScalarGridSpec(
            num_scalar_prefetch=1,
            grid=(b, t // SWA_ROWS),
            in_specs=[tile, _resident((1, D_MODEL)), _resident((D_MODEL, n_qkv)), _resident((1, n_qkv)),
                      table, table, _resident((SWA_Q, D_MODEL)), _resident((1, D_MODEL))],
            out_specs=tile,
            scratch_shapes=[pltpu.VMEM(pad_shape, BF16), pltpu.VMEM(pad_shape, BF16),
                            pltpu.VMEM((SWA_ROWS, SWA_Q), BF16)]),
        out_shape=jax.ShapeDtypeStruct(x.shape, F32),
        compiler_params=pltpu.CompilerParams(dimension_semantics=("arbitrary", "arbitrary"),
                                             vmem_limit_bytes=VMEM_LIMIT_BYTES),
        name="swa_mixer",
    )(sinks.astype(F32), x, norm_g, w_qkv16, b_qkv.reshape(1, n_qkv).astype(F32), cos_t, sin_t,
      w_o16, b_o.reshape(1, D_MODEL).astype(F32))


def kernel(x, ffn1_norm, ffn1_w_gate, ffn1_w_up, ffn1_w_down, mix_norm, ffn2_norm, ffn2_w_gate, ffn2_w_up,
           ffn2_w_down, hyb_w_in, gdn_conv_w, gdn_a_log, gdn_dt_bias, gdn_out_norm, sc_conv_w, hyb_w_out,
           swa_w_qkv, swa_b_qkv, swa_sinks, swa_w_o, swa_b_o, final_norm):
    b, t, d = x.shape
    depth = ffn1_norm.shape[0]
    final_g = final_norm.reshape(1, d).astype(F32)
    ffn1 = [w.astype(F32) for w in (ffn1_w_gate, ffn1_w_up, ffn1_w_down)]
    ffn2 = [w.astype(F32) for w in (ffn2_w_gate, ffn2_w_up, ffn2_w_down)]
    hyb_in16, hyb_out16 = hyb_w_in.astype(BF16), hyb_w_out.astype(BF16)
    swa_qkv16, swa_o16 = swa_w_qkv.astype(BF16), swa_w_o.astype(BF16)

    def ffn(xin, norm, ws, li, last):
        y = _ffn(xin.reshape(b * t, d), norm.reshape(1, d).astype(F32), ws[0], ws[1], ws[2], li, final_g, last)
        return y.reshape(b, t, d)

    for li in range(depth):
        x = ffn(x, ffn1_norm[li], ffn1, li, False)
        norm = mix_norm[li].reshape(1, d).astype(F32)
        j = li // 2
        if li % 2 == 0:
            x = _hybrid(x, norm, hyb_in16[j], gdn_conv_w[j], gdn_a_log[j], gdn_dt_bias[j], gdn_out_norm[j],
                        sc_conv_w[j], hyb_out16[j])
        else:
            x = _swa(x, norm, swa_qkv16[j], swa_b_qkv[j], swa_sinks[j], swa_o16[j], swa_b_o[j])
        x = ffn(x, ffn2_norm[li], ffn2, li, li == depth - 1)
    return x
```

```python
import functools

import jax
import jax.numpy as jnp
from jax import lax
from jax.experimental import pallas as pl
from jax.experimental.pallas import tpu as pltpu

F32 = jnp.float32
BF16 = jnp.bfloat16

D_MODEL = 1024
D_FF = 2816
GDN_HEADS = 4
GDN_HEAD_DIM = 128
GDN_CONV = 4
GDN_QKV = 3 * GDN_HEADS * GDN_HEAD_DIM
GDN_Z = GDN_HEADS * GDN_HEAD_DIM
SC_CHANNELS = 512
SC_CONV = 3
SWA_HEADS = 16
SWA_KV_HEADS = 4
SWA_HEAD_DIM = 64
SWA_WINDOW = 128
SWA_Q = SWA_HEADS * SWA_HEAD_DIM
SWA_KV = SWA_KV_HEADS * SWA_HEAD_DIM
ROPE_THETA = 10000.0
RMS_EPS = 1e-6
L2_EPS = 1e-6

LANES = 128
SUBLANES = 8
VMEM_LIMIT_BYTES = 56 * 1024 * 1024

FFN_ROWS = 1024
FFN_PARTS = 4
FFN_W_CHUNKS = 8
FFN_COLS = 256
HYB_ROWS = 512
HYB_OUT_COLS = 256
HYB_ZSC_COLS = 256
HYB_ZSC_EARLY = 8
HYB_ZSC_MID = 0
GDN_UNIT = 128
SWA_ROWS = 1024
SWA_PART = 256
SWA_LOOKAHEAD = 1
STREAM_BUFFERS = 3
CONV_HALO = SUBLANES
ROPE_SPAN = 128

NEG_INF = float("-inf")


def _rms(x, g):
    ms = jnp.mean(x * x, axis=-1, keepdims=True)
    return x * lax.rsqrt(ms + RMS_EPS) * g


def _sigmoid(x):
    return 1.0 / (1.0 + jnp.exp(-x))


def _silu(x):
    return x * _sigmoid(x)


def _dot(a, b):
    return jnp.dot(a, b, preferred_element_type=F32)


def _dot_nt(a, b):
    return lax.dot_general(a, b, (((1,), (1,)), ((), ())), preferred_element_type=F32)


def _dot_tn(a, b):
    return lax.dot_general(a, b, (((0,), (0,)), ((), ())), preferred_element_type=F32)


def _resident(shape):
    nd = len(shape)
    return pl.BlockSpec(shape, lambda *_: (0,) * nd, pipeline_mode=pl.Buffered(1))


def _resident_layer(shape, layer):
    nd = len(shape)
    return pl.BlockSpec((pl.squeezed,) + tuple(shape), lambda *_: (layer,) + (0,) * nd,
                        pipeline_mode=pl.Buffered(1))


def _load_cast(src_hbm, dst_ref, stage_ref, sem_ref):
    chunk = stage_ref.shape[1]
    n_chunks = src_hbm.shape[0] // chunk
    assert n_chunks * chunk == src_hbm.shape[0]

    def copy(k, slot):
        rows = pl.ds(pl.multiple_of(k * chunk, chunk), chunk)
        return pltpu.make_async_copy(src_hbm.at[rows, :], stage_ref.at[slot], sem_ref.at[slot])

    copy(0, 0).start()

    def body(k, carry):
        slot = k % 2

        @pl.when(k + 1 < n_chunks)
        def _():
            copy(k + 1, 1 - slot).start()

        copy(k, slot).wait()
        dst_ref[pl.ds(pl.multiple_of(k * chunk, chunk), chunk), :] = stage_ref[slot].astype(BF16)
        return carry

    lax.fori_loop(0, n_chunks, body, 0)


def _ffn_body(x_ref, g_ref, wg_hbm, wu_hbm, wd_hbm, fg_ref, o_ref, act_ref, wg_ref, wu_ref, wd_ref,
              stage_in_ref, stage_out_ref, sem_ref, *, layer, final_norm):
    @pl.when(pl.program_id(0) == 0)
    def _():
        _load_cast(wg_hbm.at[layer], wg_ref, stage_in_ref, sem_ref)
        _load_cast(wu_hbm.at[layer], wu_ref, stage_in_ref, sem_ref)
        _load_cast(wd_hbm.at[layer], wd_ref, stage_out_ref, sem_ref)

    part = x_ref.shape[0] // FFN_PARTS
    halves = [slice(i * part, (i + 1) * part) for i in range(FFN_PARTS)]
    hs = [_rms(x_ref[rs, :], g_ref[...]).astype(BF16) for rs in halves]
    for c in range(D_FF // FFN_COLS):
        cols = slice(c * FFN_COLS, (c + 1) * FFN_COLS)
        for h, rs in zip(hs, halves):
            gate = _dot(h, wg_ref[:, cols])
            up = _dot(h, wu_ref[:, cols])
            act_ref[rs, cols] = (_silu(gate) * up).astype(BF16)
    for rs in halves:
        y = x_ref[rs, :] + 0.5 * _dot(act_ref[rs, :], wd_ref[...])
        if final_norm:
            y = _rms(y, fg_ref[...])
        o_ref[rs, :] = y


def _ffn(x2d, norm_g, wg, wu, wd, layer, final_g, final_norm):
    rows = x2d.shape[0]
    assert rows % FFN_ROWS == 0
    row_spec = pl.BlockSpec((FFN_ROWS, D_MODEL), lambda i: (i, 0))
    hbm = pl.BlockSpec(memory_space=pl.ANY)
    return pl.pallas_call(
        functools.partial(_ffn_body, layer=layer, final_norm=final_norm),
        grid=(rows // FFN_ROWS,),
        in_specs=[row_spec,_resident((1, D_MODEL)), hbm, hbm, hbm, _resident((1, D_MODEL))],
        out_specs=row_spec,
        out_shape=jax.ShapeDtypeStruct((rows, D_MODEL), F32),
        scratch_shapes=[pltpu.VMEM((FFN_ROWS, D_FF), BF16),
                        pltpu.VMEM((D_MODEL, D_FF), BF16), pltpu.VMEM((D_MODEL, D_FF), BF16),
                        pltpu.VMEM((D_FF, D_MODEL), BF16),
                        pltpu.VMEM((2, D_MODEL // FFN_W_CHUNKS, D_FF), F32),
                        pltpu.VMEM((2, D_FF // FFN_W_CHUNKS, D_MODEL), F32),
                        pltpu.SemaphoreType.DMA((2,))],
        compiler_params=pltpu.CompilerParams(dimension_semantics=("arbitrary",),
                                             vmem_limit_bytes=VMEM_LIMIT_BYTES),
        name="ffn_final" if final_norm else "ffn",
    )(x2d, norm_g, wg, wu, wd, final_g)


def _causal_conv(halo_ref, cur, w_ref, width, cs):
    rows = cur.shape[0]
    ext = jnp.concatenate([halo_ref[:, cs], cur], axis=0)
    halo_ref[:, cs] = cur[rows - CONV_HALO:, :]
    acc = ext * w_ref[0:1, cs]
    for j in range(1, width):
        acc = pltpu.roll(acc, 1, axis=0) + ext * w_ref[j:j + 1, cs]
    return acc[CONV_HALO:, :]


def _segment_cumsum(x, seg):
    pos = lax.broadcasted_iota(jnp.int32, x.shape, 0) % seg
    shift = 1
    while shift < seg:
        x = x + jnp.where(pos >= shift, pltpu.roll(x, shift, axis=0), 0.0)
        shift *= 2
    return x


def _hyb_body(x_hbm, g_ref, wqkv_ref, wzsc_ref, wba_ref, convw_ref, alog_ref, dtb_ref, onorm_ref, scw_ref,
              wout_ref, o_ref, xbuf_ref, xsem_ref, halo_ref, halo2_ref, state_ref):
    first = pl.program_id(1) == 0

    @pl.when(first)
    def _():
        state_ref[...] = jnp.zeros(state_ref.shape, F32)
        halo_ref[...] = jnp.zeros(halo_ref.shape, F32)
        halo2_ref[...] = jnp.zeros(halo2_ref.shape, F32)

    rows = xbuf_ref.shape[1]
    n_t = pl.num_programs(1)
    step = pl.program_id(0) * n_t + pl.program_id(1)
    n_steps = pl.num_programs(0) * n_t
    ahead = STREAM_BUFFERS - 1

    def x_copy(s):
        slot = s % STREAM_BUFFERS
        src_rows = pl.ds(pl.multiple_of((s % n_t) * rows, rows), rows)
        return pltpu.make_async_copy(x_hbm.at[s // n_t, src_rows, :], xbuf_ref.at[slot], xsem_ref.at[slot])

    @pl.when(step == 0)
    def _():
        for s in range(ahead):
            x_copy(s).start()

    @pl.when(step + ahead < n_steps)
    def _():
        x_copy(step + ahead).start()

    x_copy(step).wait()
    slot = step % STREAM_BUFFERS

    hd = GDN_HEAD_DIM
    unit = GDN_UNIT
    x = xbuf_ref[slot]
    h = _rms(x, g_ref[...]).astype(BF16)
    p_ba = _dot(h, wba_ref[...])
    p_qkv = _dot(h, wqkv_ref[...])
    zsc = {}

    def project_zsc(piece):
        cs = slice(piece * HYB_ZSC_COLS, (piece + 1) * HYB_ZSC_COLS)
        zsc[piece] = _dot(h, wzsc_ref[:, cs])

    for piece in range(HYB_ZSC_EARLY):
        project_zsc(piece)

    def zsc_cols(rs, start, width):
        piece, off = divmod(start, HYB_ZSC_COLS)
        assert off + width <= HYB_ZSC_COLS
        return zsc[piece][rs, off:off + width]

    beta_all = _sigmoid(p_ba)
    sp_in = p_ba + dtb_ref[...]
    softplus = jnp.maximum(sp_in, 0.0) + jnp.log1p(jnp.exp(-jnp.abs(sp_in)))
    g_all = -jnp.exp(alog_ref[...]) * softplus
    gc_all = _segment_cumsum(g_all, unit)
    gc_rows = gc_all.T

    cols = []
    for cg in range(GDN_QKV // hd):
        cs = slice(cg * hd, (cg + 1) * hd)
        c = _silu(_causal_conv(halo_ref, p_qkv[:, cs], convw_ref, GDN_CONV, cs))
        if cg < 2 * GDN_HEADS:
            c = c * lax.rsqrt(jnp.sum(c * c, axis=-1, keepdims=True) + L2_EPS)
        if cg < GDN_HEADS:
            c = c * (hd ** -0.5)
        cols.append(c)

    row = lax.broadcasted_iota(jnp.int32, (unit, unit), 0)
    col = lax.broadcasted_iota(jnp.int32, (unit, unit), 1)
    lower_incl = row >= col
    lower_strict = row > col
    eye = jnp.where(row == col, 1.0, 0.0)

    n_units = rows // unit
    pairs = [(u, hh) for u in range(n_units) for hh in range(GDN_HEADS)]
    pre = {}
    for u, hh in pairs:
        rs = slice(u * unit, (u + 1) * unit)
        q = cols[hh][rs]
        k = cols[GDN_HEADS + hh][rs]
        v = cols[2 * GDN_HEADS + hh][rs]
        beta = beta_all[rs, hh:hh + 1]
        gc = gc_all[rs, GDN_HEADS + hh:GDN_HEADS + hh + 1]
        gc_row = gc_rows[GDN_HEADS + hh:GDN_HEADS + hh + 1, rs]
        g_last = gc[unit - 1:unit, :]
        e_gc = jnp.exp(gc)
        kb = k * beta
        pre[u, hh] = dict(
            decay=jnp.exp(jnp.where(lower_incl, gc - gc_row, NEG_INF)),
            kbq16=jnp.concatenate([kb, q], axis=0).astype(BF16),
            k16=k.astype(BF16),
            rhs16=jnp.concatenate([v * beta, kb * e_gc], axis=-1).astype(BF16),
            qg16=(q * e_gc).astype(BF16),
            kg16=(k * jnp.exp(g_last - gc)).astype(BF16),
            s_decay=jnp.exp(g_last))

    kq = {p: _dot_nt(pre[p]["kbq16"], pre[p]["k16"]) for p in pairs}
    neg_m, attn16, inv = {}, {}, {}
    for p in pairs:
        decay = pre[p]["decay"]
        n = jnp.where(lower_strict, -(kq[p][:unit] * decay), 0.0)
        neg_m[p] = n.astype(BF16)
        attn16[p] = (kq[p][unit:] * decay).astype(BF16)
        inv[p] = eye + n
    for piece in range(HYB_ZSC_EARLY, HYB_ZSC_EARLY + HYB_ZSC_MID):
        project_zsc(piece)
    span = 2
    while span < unit:
        resid = {p: ((eye - inv[p]) + _dot(neg_m[p], inv[p].astype(BF16))).astype(BF16) for p in pairs}
        inv = {p: inv[p] + _dot(inv[p].astype(BF16), resid[p]) for p in pairs}
        span *= 2

    sol, wq, v16, gdn_out, y_sc16 = {}, {}, {}, {}, {}
    all_rows = slice(0, rows)

    def short_conv():
        y_parts = []
        for cg in range(SC_CHANNELS // LANES):
            cs = slice(cg * LANES, (cg + 1) * LANES)
            sc_b = zsc_cols(all_rows, GDN_Z + cg * LANES, LANES)
            sc_ch = (zsc_cols(all_rows, GDN_Z + SC_CHANNELS + cg * LANES, LANES)
                     * zsc_cols(all_rows, GDN_Z + 2 * SC_CHANNELS + cg * LANES, LANES))
            y_parts.append(sc_b * _causal_conv(halo2_ref, sc_ch, scw_ref, SC_CONV, cs))
        y_sc16[0] = jnp.concatenate(y_parts, axis=-1).astype(BF16)

    def solve(u):
        for hh in range(GDN_HEADS):
            sol[u, hh] = _dot(inv[u, hh].astype(BF16), pre[u, hh]["rhs16"])

    def chain_read(u):
        for hh in range(GDN_HEADS):
            p = (u, hh)
            s16 = state_ref[hh].astype(BF16)
            wq[p] = _dot(jnp.concatenate([sol[p][:, hd:].astype(BF16), pre[p]["qg16"]], axis=0), s16)

    def chain_write(u):
        for hh in range(GDN_HEADS):
            p = (u, hh)
            v16[p] = (sol[p][:, :hd] - wq[p][:unit]).astype(BF16)
            state_ref[hh] = state_ref[hh] * pre[p]["s_decay"] + _dot_tn(pre[p]["kg16"], v16[p])

    def unit_output(u):
        rs = slice(u * unit, (u + 1) * unit)
        heads = []
        for hh in range(GDN_HEADS):
            p = (u, hh)
            o = wq[p][unit:] + _dot(attn16[p], v16[p])
            o = o * lax.rsqrt(jnp.mean(o * o, axis=-1, keepdims=True) + RMS_EPS)
            heads.append(o * onorm_ref[...] * _silu(zsc_cols(rs, hh * hd, hd)))
        gdn_out[u] = jnp.concatenate(heads, axis=-1).astype(BF16)

    def project_out(u0, col):
        rs = slice(u0 * unit, (u0 + 2) * unit)
        mix = jnp.concatenate([jnp.concatenate([gdn_out[u0], gdn_out[u0 + 1]], axis=0), y_sc16[0][rs]], axis=-1)
        cs = slice(col, col + HYB_OUT_COLS)
        o_ref[0, rs, cs] = xbuf_ref[slot, rs, cs] + _dot(mix, wout_ref[:, cs])

    assert n_units % 2 == 0
    late_pieces = [functools.partial(project_zsc, piece)
                   for piece in range(HYB_ZSC_EARLY + HYB_ZSC_MID, (GDN_Z + 3 * SC_CHANNELS) // HYB_ZSC_COLS)]
    fillers = []
    for u in range(1, n_units):
        fillers.append(functools.partial(solve, u))
        if late_pieces:
            fillers.append(late_pieces.pop(0))
    fillers += late_pieces + [short_conv]
    solve(0)
    for u in range(n_units):
        if (u, 0) not in sol:
            fillers = [f for f in fillers if not (f.func is solve and f.args == (u,))]
            solve(u)
        for step in (chain_read, chain_write):
            step(u)
            if fillers:
                fillers.pop(0)()
        fillers.append(functools.partial(unit_output, u))
        if u % 2 == 1:
            fillers += [functools.partial(project_out, u - 1, col) for col in range(0, D_MODEL, HYB_OUT_COLS)]
    for f in fillers:
        f()


def _hybrid(x, norm_g, w_in16, conv_w, a_log, dt_bias, out_norm, sc_conv_w, w_out16):
    b, t, _ = x.shape
    assert t % HYB_ROWS == 0 and HYB_ROWS % GDN_UNIT == 0
    n_qkvz = GDN_QKV + GDN_Z
    n_zsc = GDN_Z + 3 * SC_CHANNELS
    w_qkv = w_in16[:, :GDN_QKV]
    w_ba = jnp.pad(w_in16[:, n_qkvz:n_qkvz + 2 * GDN_HEADS], ((0, 0), (0, LANES - 2 * GDN_HEADS)))
    w_zsc = jnp.concatenate([w_in16[:, GDN_QKV:n_qkvz], w_in16[:, n_qkvz + 2 * GDN_HEADS:]], axis=1)
    lane_pad = (GDN_HEADS, LANES - 2 * GDN_HEADS)
    alog_row = jnp.pad(a_log.astype(F32), lane_pad).reshape(1, LANES)
    dtb_row = jnp.pad(dt_bias.astype(F32), lane_pad).reshape(1, LANES)
    tile = pl.BlockSpec((1, HYB_ROWS, D_MODEL), lambda bi, ti: (bi, ti, 0))
    assert b * (t // HYB_ROWS) >= STREAM_BUFFERS - 1
    return pl.pallas_call(
        _hyb_body,
        grid=(b, t // HYB_ROWS),
        in_specs=[pl.BlockSpec(memory_space=pl.ANY), _resident((1, D_MODEL)),_resident((D_MODEL, GDN_QKV)), _resident((D_MODEL, n_zsc)),
                  _resident((D_MODEL, LANES)), _resident((GDN_CONV, GDN_QKV)), _resident((1, LANES)),
                  _resident((1, LANES)), _resident((1, GDN_HEAD_DIM)), _resident((SC_CONV, SC_CHANNELS)),
                  _resident((GDN_Z + SC_CHANNELS, D_MODEL))],
        out_specs=tile,
        out_shape=jax.ShapeDtypeStruct(x.shape, F32),
        scratch_shapes=[pltpu.VMEM((STREAM_BUFFERS, HYB_ROWS, D_MODEL), F32),
                        pltpu.SemaphoreType.DMA((STREAM_BUFFERS,)),
                        pltpu.VMEM((CONV_HALO, GDN_QKV), F32),
                        pltpu.VMEM((CONV_HALO, SC_CHANNELS), F32),
                        pltpu.VMEM((GDN_HEADS, GDN_HEAD_DIM, GDN_HEAD_DIM), F32)],
        compiler_params=pltpu.CompilerParams(dimension_semantics=("arbitrary", "arbitrary"),
                                             vmem_limit_bytes=VMEM_LIMIT_BYTES),
        name="hybrid_mixer",
    )(x, norm_g, w_qkv, w_zsc, w_ba, conv_w.astype(F32), alog_row, dtb_row,
      out_norm.reshape(1, GDN_HEAD_DIM).astype(F32), sc_conv_w.astype(F32), w_out16)


def _rope_body(inv_ref, cos_ref, sin_ref, cos_lo_ref, sin_lo_ref):
    rows = cos_ref.shape[0]
    half = SWA_HEAD_DIM // 2
    inv = inv_ref[...]
    lane = lax.broadcasted_iota(jnp.int32, (1, LANES), 1)
    sign = jnp.where(lane % SWA_HEAD_DIM < half, -1.0, 1.0)

    @pl.when(pl.program_id(0) == 0)
    def _():
        low = lax.broadcasted_iota(jnp.int32, (ROPE_SPAN, LANES), 0).astype(F32) * inv
        cos_lo_ref[...] = jnp.cos(low)
        sin_lo_ref[...] = jnp.sin(low)

    n_hi = rows // ROPE_SPAN
    a = lax.broadcasted_iota(jnp.int32, (n_hi, LANES), 0) + pl.program_id(0) * n_hi
    high = (a * ROPE_SPAN).astype(F32) * inv
    cos_hi, sin_hi = jnp.cos(high), jnp.sin(high)
    cos_lo, sin_lo = cos_lo_ref[...], sin_lo_ref[...]
    for i in range(n_hi):
        rs = slice(i * ROPE_SPAN, (i + 1) * ROPE_SPAN)
        c, s = cos_hi[i:i + 1, :], sin_hi[i:i + 1, :]
        cos_ref[rs, :] = c * cos_lo - s * sin_lo
        sin_ref[rs, :] = sign * (s * cos_lo + c * sin_lo)


def _rope_tables(t):
    half = SWA_HEAD_DIM // 2
    inv = ROPE_THETA ** (-jnp.arange(half, dtype=F32) / half)
    inv_row = jnp.tile(inv, LANES // half).reshape(1, LANES)
    rows = min(t, 1024)
    assert t % rows == 0 and rows % ROPE_SPAN == 0
    spec = pl.BlockSpec((rows, LANES), lambda i: (i, 0))
    return pl.pallas_call(
        _rope_body,
        grid=(t // rows,),
        in_specs=[pl.BlockSpec((1, LANES), lambda i: (0, 0))],
        out_specs=[spec, spec],
        out_shape=[jax.ShapeDtypeStruct((t, LANES), F32)] * 2,
        scratch_shapes=[pltpu.VMEM((ROPE_SPAN, LANES), F32)] * 2,
        compiler_params=pltpu.CompilerParams(dimension_semantics=("arbitrary",)),
        name="rope_tables",
    )(inv_row)


def _rope_apply(x, cos, sin_signed):
    half = SWA_HEAD_DIM // 2
    lane = lax.broadcasted_iota(jnp.int32, (x.shape[0], LANES), 1)
    first_half = lane % SWA_HEAD_DIM < half
    out = []
    for c in range(x.shape[1] // LANES):
        col = x[:, c * LANES:(c + 1) * LANES]
        swapped = jnp.where(first_half,
                            pltpu.roll(col, LANES - half, axis=1),
                            pltpu.roll(col, half, axis=1))
        out.append(col * cos + swapped * sin_signed)
    return jnp.concatenate(out, axis=1)


def _swa_body(sinks_ref, x_ref, g_ref, wqkv_ref, bqkv_ref, cos_ref, sin_ref, wo_ref, bo_ref, o_ref,
              kpad_ref, vpad_ref, attn_ref):
    first = pl.program_id(1) == 0
    rows = x_ref.shape[1]
    win = SWA_WINDOW
    d = SWA_HEAD_DIM
    grp = SWA_HEADS // SWA_KV_HEADS
    n_items = 2 * SWA_KV_HEADS

    @pl.when(first)
    def _():
        kpad_ref[:, 0:win, :] = jnp.zeros((n_items, win, LANES), BF16)
        vpad_ref[:, 0:win, :] = jnp.zeros((n_items, win, LANES), BF16)

    low_half = lax.broadcasted_iota(jnp.int32, (SWA_PART, LANES), 1) < d

    def project(part):
        r0 = part * SWA_PART
        h = _rms(x_ref[0, r0:r0 + SWA_PART, :], g_ref[...]).astype(BF16)
        qkv = _dot(h, wqkv_ref[...]) + bqkv_ref[...]
        cos = cos_ref[r0:r0 + SWA_PART, :]
        sin = sin_ref[r0:r0 + SWA_PART, :]
        q = (_rope_apply(qkv[:, :SWA_Q], cos, sin) * (d ** -0.5)).astype(BF16)
        k = _rope_apply(qkv[:, SWA_Q:SWA_Q + SWA_KV], cos, sin).astype(BF16)
        v = qkv[:, SWA_Q + SWA_KV:].astype(BF16)
        dst_rows = slice(win + r0, win + r0 + SWA_PART)
        for src, dst in ((k, kpad_ref), (v, vpad_ref)):
            for pair in range(SWA_KV // LANES):
                col = src[:, pair * LANES:(pair + 1) * LANES]
                swapped = jnp.concatenate([col[:, d:], col[:, :d]], axis=1)
                zero = jnp.zeros_like(col)
                dst[4 * pair + 0, dst_rows, :] = jnp.where(low_half, col, zero)
                dst[4 * pair + 1, dst_rows, :] = jnp.where(low_half, zero, swapped)
                dst[4 * pair + 2, dst_rows, :] = jnp.where(low_half, swapped, zero)
                dst[4 * pair + 3, dst_rows, :] = jnp.where(low_half, zero, col)
        return q

    items = [(kv, par) for kv in range(SWA_KV_HEADS) for par in range(2)]

    qi = lax.broadcasted_iota(jnp.int32, (2 * win, win), 0) % win
    kj = lax.broadcasted_iota(jnp.int32, (2 * win, win), 1)
    from_prev = kj > qi
    srow = lax.broadcasted_iota(jnp.int32, (2 * win, 1), 0)
    prev_bias = jnp.where(first, NEG_INF, 0.0)

    def qk_stage(q, blk):
        lr = (blk * win) % SWA_PART
        out = {}
        for kv, par in items:
            c0 = kv * grp * d
            q2 = jnp.concatenate([q[lr:lr + win, c0:c0 + LANES], q[lr:lr + win, c0 + LANES:c0 + 2 * LANES]], axis=0)
            s = _dot_nt(q2, kpad_ref[2 * kv + par, blk * win:(blk + 2) * win, :])
            s_prev = s[:, :win] + prev_bias if blk == 0 else s[:, :win]
            out[kv, par] = jnp.where(from_prev, s_prev, s[:, win:])
        return out

    def softmax_pv_stage(scores, blk):
        probs = {}
        for kv, par in items:
            folded = scores[kv, par]
            sink = jnp.where(srow < win, sinks_ref[kv * grp + par], sinks_ref[kv * grp + 2 + par])
            mx = jnp.maximum(jnp.max(folded, axis=-1, keepdims=True), sink)
            e = jnp.exp(folded - mx)
            inv_den = 1.0 / (jnp.sum(e, axis=-1, keepdims=True) + jnp.exp(sink - mx))
            p = (e * inv_den).astype(BF16)
            zero = jnp.zeros_like(p)
            probs[kv, par] = jnp.concatenate([jnp.where(from_prev, p, zero), jnp.where(from_prev, zero, p)], axis=1)
        for kv in range(SWA_KV_HEADS):
            c0 = kv * grp * d
            out = (_dot(probs[kv, 0], vpad_ref[2 * kv, blk * win:(blk + 2) * win, :])
                   + _dot(probs[kv, 1], vpad_ref[2 * kv + 1, blk * win:(blk + 2) * win, :]))
            attn_ref[blk * win:(blk + 1) * win, c0:c0 + LANES] = out[:win].astype(BF16)
            attn_ref[blk * win:(blk + 1) * win, c0 + LANES:c0 + 2 * LANES] = out[win:].astype(BF16)

    n_parts = rows // SWA_PART
    blk_per_part = SWA_PART // win
    qs = {part: project(part) for part in range(min(SWA_LOOKAHEAD, n_parts))}
    scores = {blk: qk_stage(qs[0], blk) for blk in range(blk_per_part)}
    for part in range(n_parts):
        nxt = part + 1
        if part + SWA_LOOKAHEAD < n_parts:
            qs[part + SWA_LOOKAHEAD] = project(part + SWA_LOOKAHEAD)
        for blk in range(part * blk_per_part, nxt * blk_per_part):
            softmax_pv_stage(scores.pop(blk), blk)
        if nxt < n_parts:
            for blk in range(nxt * blk_per_part, (nxt + 1) * blk_per_part):
                scores[blk] = qk_stage(qs[nxt], blk)
        rs = slice(part * SWA_PART, nxt * SWA_PART)
        o_ref[0, rs, :] = x_ref[0, rs, :] + _dot(attn_ref[rs, :], wo_ref[...]) + bo_ref[...]

    kpad_ref[:, 0:win, :] = kpad_ref[:, rows:rows + win, :]
    vpad_ref[:, 0:win, :] = vpad_ref[:, rows:rows + win, :]


def _swa(x, norm_g, w_qkv16, b_qkv, sinks, w_o16, b_o):
    b, t, _ = x.shape
    assert t % SWA_ROWS == 0 and SWA_ROWS % SWA_PART == 0 and SWA_PART % SWA_WINDOW == 0
    cos_t, sin_t = _rope_tables(t)
    n_qkv = SWA_Q + 2 * SWA_KV
    tile = pl.BlockSpec((1, SWA_ROWS, D_MODEL), lambda bi, ti, *_: (bi, ti, 0))
    table =pl.BlockSpec((SWA_ROWS, LANES), lambda bi, ti, *_: (ti, 0))
    pad_shape = (2 * SWA_KV_HEADS, SWA_WINDOW + SWA_ROWS, LANES)
    return pl.pallas_call(
        _swa_body,
        grid_spec=pltpu.PrefetchScalarGridSpec(
            num_scalar_prefetch=1,
            grid=(b, t // SWA_ROWS),
            in_specs=[tile, _resident((1, D_MODEL)), _resident((D_MODEL, n_qkv)), _resident((1, n_qkv)),
                      table, table, _resident((SWA_Q, D_MODEL)), _resident((1, D_MODEL))],
            out_specs=tile,
            scratch_shapes=[pltpu.VMEM(pad_shape, BF16), pltpu.VMEM(pad_shape, BF16),
                            pltpu.VMEM((SWA_ROWS, SWA_Q), BF16)]),
        out_shape=jax.ShapeDtypeStruct(x.shape, F32),
        compiler_params=pltpu.CompilerParams(dimension_semantics=("arbitrary", "arbitrary"),
                                             vmem_limit_bytes=VMEM_LIMIT_BYTES),
        name="swa_mixer",
    )(sinks.astype(F32), x, norm_g, w_qkv16, b_qkv.reshape(1, n_qkv).astype(F32), cos_t, sin_t,
      w_o16, b_o.reshape(1, D_MODEL).astype(F32))


def kernel(x, ffn1_norm, ffn1_w_gate, ffn1_w_up, ffn1_w_down, mix_norm, ffn2_norm, ffn2_w_gate, ffn2_w_up,
           ffn2_w_down, hyb_w_in, gdn_conv_w, gdn_a_log, gdn_dt_bias, gdn_out_norm, sc_conv_w, hyb_w_out,
           swa_w_qkv, swa_b_qkv, swa_sinks, swa_w_o, swa_b_o, final_norm):
    b, t, d = x.shape
    depth = ffn1_norm.shape[0]
    final_g = final_norm.reshape(1, d).astype(F32)
    ffn1 = [w.astype(F32) for w in (ffn1_w_gate, ffn1_w_up, ffn1_w_down)]
    ffn2 = [w.astype(F32) for w in (ffn2_w_gate, ffn2_w_up, ffn2_w_down)]
    hyb_in16, hyb_out16 = hyb_w_in.astype(BF16), hyb_w_out.astype(BF16)
    swa_qkv16, swa_o16 = swa_w_qkv.astype(BF16), swa_w_o.astype(BF16)

    def ffn(xin, norm, ws, li, last):
        y = _ffn(xin.reshape(b * t, d), norm.reshape(1, d).astype(F32), ws[0], ws[1], ws[2], li, final_g, last)
        return y.reshape(b, t, d)

    for li in range(depth):
        x = ffn(x, ffn1_norm[li], ffn1, li, False)
        norm = mix_norm[li].reshape(1, d).astype(F32)
        j = li // 2
        if li % 2 == 0:
            x = _hybrid(x, norm, hyb_in16[j], gdn_conv_w[j], gdn_a_log[j], gdn_dt_bias[j], gdn_out_norm[j],
                        sc_conv_w[j], hyb_out16[j])
        else:
            x = _swa(x, norm, swa_qkv16[j], swa_b_qkv[j], swa_sinks[j], swa_o16[j], swa_b_o[j])
        x = ffn(x, ffn2_norm[li], ffn2, li, li == depth - 1)
    return x
```

```python
import functools

import jax
import jax.numpy as jnp
from jax import lax
from jax.experimental import pallas as pl
from jax.experimental.pallas import tpu as pltpu

F32 = jnp.float32
BF16 = jnp.bfloat16

D_MODEL = 1024
D_FF = 2816
GDN_HEADS = 4
GDN_HEAD_DIM = 128
GDN_CONV = 4
GDN_QKV = 3 * GDN_HEADS * GDN_HEAD_DIM
GDN_Z = GDN_HEADS * GDN_HEAD_DIM
SC_CHANNELS = 512
SC_CONV = 3
SWA_HEADS = 16
SWA_KV_HEADS = 4
SWA_HEAD_DIM = 64
SWA_WINDOW = 128
SWA_Q = SWA_HEADS * SWA_HEAD_DIM
SWA_KV = SWA_KV_HEADS * SWA_HEAD_DIM
ROPE_THETA = 10000.0
RMS_EPS = 1e-6
L2_EPS = 1e-6

LANES = 128
SUBLANES = 8
VMEM_LIMIT_BYTES = 56 * 1024 * 1024

FFN_ROWS = 1024
FFN_PARTS = 4
FFN_W_CHUNKS = 8
FFN_COLS = 256
HYB_ROWS = 512
HYB_OUT_COLS = 256
HYB_ZSC_COLS = 256
HYB_ZSC_EARLY = 8
HYB_ZSC_MID = 0
GDN_UNIT = 128
SWA_ROWS = 1024
SWA_PART = 256
SWA_LOOKAHEAD = 1
CONV_HALO = SUBLANES
ROPE_SPAN = 128

NEG_INF = float("-inf")


def _rms(x, g):
    ms = jnp.mean(x * x, axis=-1, keepdims=True)
    return x * lax.rsqrt(ms + RMS_EPS) * g


def _sigmoid(x):
    return 1.0 / (1.0 + jnp.exp(-x))


def _silu(x):
    return x * _sigmoid(x)


def _dot(a, b):
    return jnp.dot(a, b, preferred_element_type=F32)


def _dot_nt(a, b):
    return lax.dot_general(a, b, (((1,), (1,)), ((), ())), preferred_element_type=F32)


def _dot_tn(a, b):
    return lax.dot_general(a, b, (((0,), (0,)), ((), ())), preferred_element_type=F32)


def _resident(shape):
    nd = len(shape)
    return pl.BlockSpec(shape, lambda *_: (0,) * nd, pipeline_mode=pl.Buffered(1))


def _resident_layer(shape, layer):
    nd = len(shape)
    return pl.BlockSpec((pl.squeezed,) + tuple(shape), lambda *_: (layer,) + (0,) * nd,
                        pipeline_mode=pl.Buffered(1))


def _load_cast(src_hbm, dst_ref, stage_ref, sem_ref):
    chunk = stage_ref.shape[1]
    n_chunks = src_hbm.shape[0] // chunk
    assert n_chunks * chunk == src_hbm.shape[0]

    def copy(k, slot):
        rows = pl.ds(pl.multiple_of(k * chunk, chunk), chunk)
        return pltpu.make_async_copy(src_hbm.at[rows, :], stage_ref.at[slot], sem_ref.at[slot])

    copy(0, 0).start()

    def body(k, carry):
        slot = k % 2

        @pl.when(k + 1 < n_chunks)
        def _():
            copy(k + 1, 1 - slot).start()

        copy(k, slot).wait()
        dst_ref[pl.ds(pl.multiple_of(k * chunk, chunk), chunk), :] = stage_ref[slot].astype(BF16)
        return carry

    lax.fori_loop(0, n_chunks, body, 0)


def _ffn_body(x_ref, g_ref, wg_hbm, wu_hbm, wd_hbm, fg_ref, o_ref, act_ref, wg_ref, wu_ref, wd_ref,
              stage_in_ref, stage_out_ref, sem_ref, *, layer, final_norm):
    @pl.when(pl.program_id(0) == 0)
    def _():
        _load_cast(wg_hbm.at[layer], wg_ref, stage_in_ref, sem_ref)
        _load_cast(wu_hbm.at[layer], wu_ref, stage_in_ref, sem_ref)
        _load_cast(wd_hbm.at[layer], wd_ref, stage_out_ref, sem_ref)

    part = x_ref.shape[0] // FFN_PARTS
    halves = [slice(i * part, (i + 1) * part) for i in range(FFN_PARTS)]
    hs = [_rms(x_ref[rs, :], g_ref[...]).astype(BF16) for rs in halves]
    for c in range(D_FF // FFN_COLS):
        cols = slice(c * FFN_COLS, (c + 1) * FFN_COLS)
        for h, rs in zip(hs, halves):
            gate = _dot(h, wg_ref[:, cols])
            up = _dot(h, wu_ref[:, cols])
            act_ref[rs, cols] = (_silu(gate) * up).astype(BF16)
    for rs in halves:
        y = x_ref[rs, :] + 0.5 * _dot(act_ref[rs, :], wd_ref[...])
        if final_norm:
            y = _rms(y, fg_ref[...])
        o_ref[rs, :] = y


def _ffn(x2d, norm_g, wg, wu, wd, layer, final_g, final_norm):
    rows = x2d.shape[0]
    assert rows % FFN_ROWS == 0
    row_spec = pl.BlockSpec((FFN_ROWS, D_MODEL), lambda i: (i, 0))
    hbm = pl.BlockSpec(memory_space=pl.ANY)
    return pl.pallas_call(
        functools.partial(_ffn_body, layer=layer, final_norm=final_norm),
        grid=(rows // FFN_ROWS,),
        in_specs=[row_spec, _resident((1, D_MODEL)), hbm, hbm, hbm, _resident((1, D_MODEL))],
        out_specs=row_spec,
        out_shape=jax.ShapeDtypeStruct((rows, D_MODEL), F32),
        scratch_shapes=[pltpu.VMEM((FFN_ROWS, D_FF), BF16),
                        pltpu.VMEM((D_MODEL, D_FF), BF16), pltpu.VMEM((D_MODEL, D_FF), BF16),
                        pltpu.VMEM((D_FF, D_MODEL), BF16),
                        pltpu.VMEM((2, D_MODEL // FFN_W_CHUNKS, D_FF), F32),
                        pltpu.VMEM((2, D_FF // FFN_W_CHUNKS, D_MODEL), F32),
                        pltpu.SemaphoreType.DMA((2,))],
        compiler_params=pltpu.CompilerParams(dimension_semantics=("arbitrary",),
                                             vmem_limit_bytes=VMEM_LIMIT_BYTES),
        name="ffn_final" if final_norm else "ffn",
    )(x2d, norm_g, wg, wu, wd, final_g)


def _causal_conv(halo_ref, cur, w_ref, width, cs):
    rows = cur.shape[0]
    ext = jnp.concatenate([halo_ref[:, cs], cur], axis=0)
    halo_ref[:, cs] = cur[rows - CONV_HALO:, :]
    acc = ext * w_ref[0:1, cs]
    for j in range(1, width):
        acc = pltpu.roll(acc, 1, axis=0) + ext * w_ref[j:j + 1, cs]
    return acc[CONV_HALO:, :]


def _segment_cumsum(x, seg):
    pos = lax.broadcasted_iota(jnp.int32, x.shape, 0) % seg
    shift = 1
    while shift < seg:
        x = x + jnp.where(pos >= shift, pltpu.roll(x, shift, axis=0), 0.0)
        shift *= 2
    return x


def _hyb_body(x_ref, g_ref, wqkv_ref, wzsc_ref, wba_ref, convw_ref, alog_ref, dtb_ref, onorm_ref, scw_ref,
              wout_ref, o_ref, halo_ref, halo2_ref, state_ref):
    first = pl.program_id(1) == 0

    @pl.when(first)
    def _():
        state_ref[...] = jnp.zeros(state_ref.shape, F32)
        halo_ref[...] = jnp.zeros(halo_ref.shape, F32)
        halo2_ref[...] = jnp.zeros(halo2_ref.shape, F32)

    rows = x_ref.shape[1]
    hd = GDN_HEAD_DIM
    unit = GDN_UNIT
    x = x_ref[0]
    h = _rms(x, g_ref[...]).astype(BF16)
    p_ba = _dot(h, wba_ref[...])
    p_qkv = _dot(h, wqkv_ref[...])
    zsc = {}

    def project_zsc(piece):
        cs = slice(piece * HYB_ZSC_COLS, (piece + 1) * HYB_ZSC_COLS)
        zsc[piece] = _dot(h, wzsc_ref[:, cs])

    for piece in range(HYB_ZSC_EARLY):
        project_zsc(piece)

    def zsc_cols(rs, start, width):
        piece, off = divmod(start, HYB_ZSC_COLS)
        assert off + width <= HYB_ZSC_COLS
        return zsc[piece][rs, off:off + width]

    beta_all = _sigmoid(p_ba)
    sp_in = p_ba + dtb_ref[...]
    softplus = jnp.maximum(sp_in, 0.0) + jnp.log1p(jnp.exp(-jnp.abs(sp_in)))
    g_all = -jnp.exp(alog_ref[...]) * softplus
    gc_all = _segment_cumsum(g_all, unit)
    gc_rows = gc_all.T

    cols = []
    for cg in range(GDN_QKV // hd):
        cs = slice(cg * hd, (cg + 1) * hd)
        c = _silu(_causal_conv(halo_ref, p_qkv[:, cs], convw_ref, GDN_CONV, cs))
        if cg < 2 * GDN_HEADS:
            c = c * lax.rsqrt(jnp.sum(c * c, axis=-1, keepdims=True) + L2_EPS)
        if cg < GDN_HEADS:
            c = c * (hd ** -0.5)
        cols.append(c)

    row = lax.broadcasted_iota(jnp.int32, (unit, unit), 0)
    col = lax.broadcasted_iota(jnp.int32, (unit, unit), 1)
    lower_incl = row >= col
    lower_strict = row > col
    eye = jnp.where(row == col, 1.0, 0.0)

    n_units = rows // unit
    pairs = [(u, hh) for u in range(n_units) for hh in range(GDN_HEADS)]
    pre = {}
    for u, hh in pairs:
        rs = slice(u * unit, (u + 1) * unit)
        q = cols[hh][rs]
        k = cols[GDN_HEADS + hh][rs]
        v = cols[2 * GDN_HEADS + hh][rs]
        beta = beta_all[rs, hh:hh + 1]
        gc = gc_all[rs, GDN_HEADS + hh:GDN_HEADS + hh + 1]
        gc_row = gc_rows[GDN_HEADS + hh:GDN_HEADS + hh + 1, rs]
        g_last = gc[unit - 1:unit, :]
        e_gc = jnp.exp(gc)
        kb = k * beta
        pre[u, hh] = dict(
            decay=jnp.exp(jnp.where(lower_incl, gc - gc_row, NEG_INF)),
            kbq16=jnp.concatenate([kb, q], axis=0).astype(BF16),
            k16=k.astype(BF16),
            rhs16=jnp.concatenate([v * beta, kb * e_gc], axis=-1).astype(BF16),
            qg16=(q * e_gc).astype(BF16),
            kg16=(k * jnp.exp(g_last - gc)).astype(BF16),
            s_decay=jnp.exp(g_last))

    kq = {p: _dot_nt(pre[p]["kbq16"], pre[p]["k16"]) for p in pairs}
    neg_m, attn16, inv = {}, {}, {}
    for p in pairs:
        decay = pre[p]["decay"]
        n = jnp.where(lower_strict, -(kq[p][:unit] * decay), 0.0)
        neg_m[p] = n.astype(BF16)
        attn16[p] = (kq[p][unit:] * decay).astype(BF16)
        inv[p] = eye + n
    for piece in range(HYB_ZSC_EARLY, HYB_ZSC_EARLY + HYB_ZSC_MID):
        project_zsc(piece)
    span = 2
    while span < unit:
        resid = {p: ((eye - inv[p]) + _dot(neg_m[p], inv[p].astype(BF16))).astype(BF16) for p in pairs}
        inv = {p: inv[p] + _dot(inv[p].astype(BF16), resid[p]) for p in pairs}
        span *= 2

    sol, wq, v16, gdn_out, y_sc16 = {}, {}, {}, {}, {}
    all_rows = slice(0, rows)

    def short_conv():
        y_parts = []
        for cg in range(SC_CHANNELS // LANES):
            cs = slice(cg * LANES, (cg + 1) * LANES)
            sc_b = zsc_cols(all_rows, GDN_Z + cg * LANES, LANES)
            sc_ch = (zsc_cols(all_rows, GDN_Z + SC_CHANNELS + cg * LANES, LANES)
                     * zsc_cols(all_rows, GDN_Z + 2 * SC_CHANNELS + cg * LANES, LANES))
            y_parts.append(sc_b * _causal_conv(halo2_ref, sc_ch, scw_ref, SC_CONV, cs))
        y_sc16[0] = jnp.concatenate(y_parts, axis=-1).astype(BF16)

    def solve(u):
        for hh in range(GDN_HEADS):
            sol[u, hh] = _dot(inv[u, hh].astype(BF16), pre[u, hh]["rhs16"])

    def chain_read(u):
        for hh in range(GDN_HEADS):
            p = (u, hh)
            s16 = state_ref[hh].astype(BF16)
            wq[p] = _dot(jnp.concatenate([sol[p][:, hd:].astype(BF16), pre[p]["qg16"]], axis=0), s16)

    def chain_write(u):
        for hh in range(GDN_HEADS):
            p = (u, hh)
            v16[p] = (sol[p][:, :hd] - wq[p][:unit]).astype(BF16)
            state_ref[hh] = state_ref[hh] * pre[p]["s_decay"] + _dot_tn(pre[p]["kg16"], v16[p])

    def unit_output(u):
        rs = slice(u * unit, (u + 1) * unit)
        heads = []
        for hh in range(GDN_HEADS):
            p = (u, hh)
            o = wq[p][unit:] + _dot(attn16[p], v16[p])
            o = o * lax.rsqrt(jnp.mean(o * o, axis=-1, keepdims=True) + RMS_EPS)
            heads.append(o * onorm_ref[...] * _silu(zsc_cols(rs, hh * hd, hd)))
        gdn_out[u] = jnp.concatenate(heads, axis=-1).astype(BF16)

    def project_out(u0, col):
        rs = slice(u0 * unit, (u0 + 2) * unit)
        mix = jnp.concatenate([jnp.concatenate([gdn_out[u0], gdn_out[u0 + 1]], axis=0), y_sc16[0][rs]], axis=-1)
        cs = slice(col, col + HYB_OUT_COLS)
        o_ref[0, rs, cs] = x_ref[0, rs, cs] + _dot(mix, wout_ref[:, cs])

    assert n_units % 2 == 0
    late_pieces = [functools.partial(project_zsc, piece)
                   for piece in range(HYB_ZSC_EARLY + HYB_ZSC_MID, (GDN_Z + 3 * SC_CHANNELS) // HYB_ZSC_COLS)]
    fillers = []
    for u in range(1, n_units):
        fillers.append(functools.partial(solve, u))
        if late_pieces:
            fillers.append(late_pieces.pop(0))
    fillers += late_pieces + [short_conv]
    solve(0)
    for u in range(n_units):
        if (u, 0) not in sol:
            fillers = [f for f in fillers if not (f.func is solve and f.args == (u,))]
            solve(u)
        for step in (chain_read, chain_write):
            step(u)
            if fillers:
                fillers.pop(0)()
        fillers.append(functools.partial(unit_output, u))
        if u % 2 == 1:
            fillers += [functools.partial(project_out, u - 1, col) for col in range(0, D_MODEL, HYB_OUT_COLS)]
    for f in fillers:
        f()


def _hybrid(x, norm_g, w_in16, conv_w, a_log, dt_bias, out_norm, sc_conv_w, w_out16):
    b, t, _ = x.shape
    assert t % HYB_ROWS == 0 and HYB_ROWS % GDN_UNIT == 0
    n_qkvz = GDN_QKV + GDN_Z
    n_zsc = GDN_Z + 3 * SC_CHANNELS
    w_qkv = w_in16[:, :GDN_QKV]
    w_ba = jnp.pad(w_in16[:, n_qkvz:n_qkvz + 2 * GDN_HEADS], ((0, 0), (0, LANES - 2 * GDN_HEADS)))
    w_zsc = jnp.concatenate([w_in16[:, GDN_QKV:n_qkvz], w_in16[:, n_qkvz + 2 * GDN_HEADS:]], axis=1)
    lane_pad = (GDN_HEADS, LANES - 2 * GDN_HEADS)
    alog_row = jnp.pad(a_log.astype(F32), lane_pad).reshape(1, LANES)
    dtb_row = jnp.pad(dt_bias.astype(F32), lane_pad).reshape(1, LANES)
    tile = pl.BlockSpec((1, HYB_ROWS, D_MODEL), lambda bi, ti: (bi, ti, 0))
    return pl.pallas_call(
        _hyb_body,
        grid=(b, t // HYB_ROWS),
        in_specs=[tile, _resident((1, D_MODEL)), _resident((D_MODEL, GDN_QKV)), _resident((D_MODEL, n_zsc)),
                  _resident((D_MODEL, LANES)), _resident((GDN_CONV, GDN_QKV)), _resident((1, LANES)),
                  _resident((1, LANES)), _resident((1, GDN_HEAD_DIM)), _resident((SC_CONV, SC_CHANNELS)),
                  _resident((GDN_Z + SC_CHANNELS, D_MODEL))],
        out_specs=tile,
        out_shape=jax.ShapeDtypeStruct(x.shape, F32),
        scratch_shapes=[pltpu.VMEM((CONV_HALO, GDN_QKV), F32),
                        pltpu.VMEM((CONV_HALO, SC_CHANNELS), F32),
                        pltpu.VMEM((GDN_HEADS, GDN_HEAD_DIM, GDN_HEAD_DIM), F32)],
        compiler_params=pltpu.CompilerParams(dimension_semantics=("arbitrary", "arbitrary"),
                                             vmem_limit_bytes=VMEM_LIMIT_BYTES),
        name="hybrid_mixer",
    )(x, norm_g, w_qkv, w_zsc, w_ba, conv_w.astype(F32), alog_row, dtb_row,
      out_norm.reshape(1, GDN_HEAD_DIM).astype(F32), sc_conv_w.astype(F32), w_out16)


def _rope_body(inv_ref, cos_ref, sin_ref, cos_lo_ref, sin_lo_ref):
    rows = cos_ref.shape[0]
    half = SWA_HEAD_DIM // 2
    inv = inv_ref[...]
    lane = lax.broadcasted_iota(jnp.int32, (1, LANES), 1)
    sign = jnp.where(lane % SWA_HEAD_DIM < half, -1.0, 1.0)

    @pl.when(pl.program_id(0) == 0)
    def _():
        low = lax.broadcasted_iota(jnp.int32, (ROPE_SPAN, LANES), 0).astype(F32) * inv
        cos_lo_ref[...] = jnp.cos(low)
        sin_lo_ref[...] = jnp.sin(low)

    n_hi = rows // ROPE_SPAN
    a = lax.broadcasted_iota(jnp.int32, (n_hi, LANES), 0) + pl.program_id(0) * n_hi
    high = (a * ROPE_SPAN).astype(F32) * inv
    cos_hi, sin_hi = jnp.cos(high), jnp.sin(high)
    cos_lo, sin_lo = cos_lo_ref[...], sin_lo_ref[...]
    for i in range(n_hi):
        rs = slice(i * ROPE_SPAN, (i + 1) * ROPE_SPAN)
        c, s = cos_hi[i:i + 1, :], sin_hi[i:i + 1, :]
        cos_ref[rs, :] = c * cos_lo - s * sin_lo
        sin_ref[rs, :] = sign * (s * cos_lo + c * sin_lo)


def _rope_tables(t):
    half = SWA_HEAD_DIM // 2
    inv = ROPE_THETA ** (-jnp.arange(half, dtype=F32) / half)
    inv_row = jnp.tile(inv, LANES // half).reshape(1, LANES)
    rows = min(t, 1024)
    assert t % rows == 0 and rows % ROPE_SPAN == 0
    spec = pl.BlockSpec((rows, LANES), lambda i: (i, 0))
    return pl.pallas_call(
        _rope_body,
        grid=(t // rows,),
        in_specs=[pl.BlockSpec((1, LANES), lambda i: (0, 0))],
        out_specs=[spec, spec],
        out_shape=[jax.ShapeDtypeStruct((t, LANES), F32)] * 2,
        scratch_shapes=[pltpu.VMEM((ROPE_SPAN, LANES), F32)] * 2,
        compiler_params=pltpu.CompilerParams(dimension_semantics=("arbitrary",)),
        name="rope_tables",
    )(inv_row)


def _rope_apply(x, cos, sin_signed):
    half = SWA_HEAD_DIM // 2
    lane = lax.broadcasted_iota(jnp.int32, (x.shape[0], LANES), 1)
    first_half = lane % SWA_HEAD_DIM < half
    out = []
    for c in range(x.shape[1] // LANES):
        col = x[:, c * LANES:(c + 1) * LANES]
        swapped = jnp.where(first_half,
                            pltpu.roll(col, LANES - half, axis=1),
                            pltpu.roll(col, half, axis=1))
        out.append(col * cos + swapped * sin_signed)
    return jnp.concatenate(out, axis=1)


def _swa_body(sinks_ref, x_ref, g_ref, wqkv_ref, bqkv_ref, inv_ref, wo_ref, bo_ref, o_ref,
              kpad_ref, vpad_ref, attn_ref, cos_lo_ref, sin_lo_ref):
    first = pl.program_id(1) == 0
    rows = x_ref.shape[1]
    win = SWA_WINDOW
    d = SWA_HEAD_DIM
    grp = SWA_HEADS // SWA_KV_HEADS
    n_items = 2 * SWA_KV_HEADS

    @pl.when(first)
    def _():
        kpad_ref[:, 0:win, :] = jnp.zeros((n_items, win, LANES), BF16)
        vpad_ref[:, 0:win, :] = jnp.zeros((n_items, win, LANES), BF16)
        low = lax.broadcasted_iota(jnp.int32, (ROPE_SPAN, LANES), 0).astype(F32) * inv_ref[...]
        cos_lo_ref[...] = jnp.cos(low)
        sin_lo_ref[...] = jnp.sin(low)

    spans = rows // ROPE_SPAN
    span_id = lax.broadcasted_iota(jnp.int32, (spans, LANES), 0) + pl.program_id(1) * spans
    high = (span_id * ROPE_SPAN).astype(F32) * inv_ref[...]
    cos_hi, sin_hi = jnp.cos(high), jnp.sin(high)
    rope_lane = lax.broadcasted_iota(jnp.int32, (1, LANES), 1)
    rope_sign = jnp.where(rope_lane % d < d // 2, -1.0, 1.0)

    def rope_tables(r0, n):
        cos_lo, sin_lo = cos_lo_ref[...], sin_lo_ref[...]
        cos_parts, sin_parts = [], []
        for i in range(r0 // ROPE_SPAN, (r0 + n) // ROPE_SPAN):
            c, s = cos_hi[i:i + 1, :], sin_hi[i:i + 1, :]
            cos_parts.append(c * cos_lo - s * sin_lo)
            sin_parts.append(rope_sign * (s * cos_lo + c * sin_lo))
        return jnp.concatenate(cos_parts, axis=0), jnp.concatenate(sin_parts, axis=0)

    low_half = lax.broadcasted_iota(jnp.int32, (SWA_PART, LANES), 1) < d

    def project(part):
        r0 = part * SWA_PART
        h = _rms(x_ref[0, r0:r0 + SWA_PART, :], g_ref[...]).astype(BF16)
        qkv = _dot(h, wqkv_ref[...]) + bqkv_ref[...]
        cos, sin = rope_tables(r0, SWA_PART)
        q = (_rope_apply(qkv[:, :SWA_Q], cos, sin) * (d ** -0.5)).astype(BF16)
        k = _rope_apply(qkv[:, SWA_Q:SWA_Q + SWA_KV], cos, sin).astype(BF16)
        v = qkv[:, SWA_Q + SWA_KV:].astype(BF16)
        dst_rows = slice(win + r0, win + r0 + SWA_PART)
        for src, dst in ((k, kpad_ref), (v, vpad_ref)):
            for pair in range(SWA_KV // LANES):
                col = src[:, pair * LANES:(pair + 1) * LANES]
                swapped = jnp.concatenate([col[:, d:], col[:, :d]], axis=1)
                zero = jnp.zeros_like(col)
                dst[4 * pair + 0, dst_rows, :] = jnp.where(low_half, col, zero)
                dst[4 * pair + 1, dst_rows, :] = jnp.where(low_half, zero, swapped)
                dst[4 * pair + 2, dst_rows, :] = jnp.where(low_half, swapped, zero)
                dst[4 * pair + 3, dst_rows, :] = jnp.where(low_half, zero, col)
        return q

    items = [(kv, par) for kv in range(SWA_KV_HEADS) for par in range(2)]

    qi = lax.broadcasted_iota(jnp.int32, (2 * win, win), 0) % win
    kj = lax.broadcasted_iota(jnp.int32, (2 * win, win), 1)
    from_prev = kj > qi
    srow = lax.broadcasted_iota(jnp.int32, (2 * win, 1), 0)
    prev_bias = jnp.where(first, NEG_INF, 0.0)

    def qk_stage(q, blk):
        lr = (blk * win) % SWA_PART
        out = {}
        for kv, par in items:
            c0 = kv * grp * d
            q2 = jnp.concatenate([q[lr:lr + win, c0:c0 + LANES], q[lr:lr + win, c0 + LANES:c0 + 2 * LANES]], axis=0)
            s = _dot_nt(q2, kpad_ref[2 * kv + par, blk * win:(blk + 2) * win, :])
            s_prev = s[:, :win] + prev_bias if blk == 0 else s[:, :win]
            out[kv, par] = jnp.where(from_prev, s_prev, s[:, win:])
        return out

    def softmax_pv_stage(scores, blk):
        probs = {}
        for kv, par in items:
            folded = scores[kv, par]
            sink = jnp.where(srow < win, sinks_ref[kv * grp + par], sinks_ref[kv * grp + 2 + par])
            mx = jnp.maximum(jnp.max(folded, axis=-1, keepdims=True), sink)
            e = jnp.exp(folded - mx)
            inv_den = 1.0 / (jnp.sum(e, axis=-1, keepdims=True) + jnp.exp(sink - mx))
            p = (e * inv_den).astype(BF16)
            zero = jnp.zeros_like(p)
            probs[kv, par] = jnp.concatenate([jnp.where(from_prev, p, zero), jnp.where(from_prev, zero, p)], axis=1)
        for kv in range(SWA_KV_HEADS):
            c0 = kv * grp * d
            out = (_dot(probs[kv, 0], vpad_ref[2 * kv, blk * win:(blk + 2) * win, :])
                   + _dot(probs[kv, 1], vpad_ref[2 * kv + 1, blk * win:(blk + 2) * win, :]))
            attn_ref[blk * win:(blk + 1) * win, c0:c0 + LANES] = out[:win].astype(BF16)
            attn_ref[blk * win:(blk + 1) * win, c0 + LANES:c0 + 2 * LANES] = out[win:].astype(BF16)

    n_parts = rows // SWA_PART
    blk_per_part = SWA_PART // win
    qs = {part: project(part) for part in range(min(SWA_LOOKAHEAD, n_parts))}
    scores = {blk: qk_stage(qs[0], blk) for blk in range(blk_per_part)}
    for part in range(n_parts):
        nxt = part + 1
        if part + SWA_LOOKAHEAD < n_parts:
            qs[part + SWA_LOOKAHEAD] = project(part + SWA_LOOKAHEAD)
        for blk in range(part * blk_per_part, nxt * blk_per_part):
            softmax_pv_stage(scores.pop(blk), blk)
        if nxt < n_parts:
            for blk in range(nxt * blk_per_part, (nxt + 1) * blk_per_part):
                scores[blk] = qk_stage(qs[nxt], blk)
        rs = slice(part * SWA_PART, nxt * SWA_PART)
        o_ref[0, rs, :] = x_ref[0, rs, :] + _dot(attn_ref[rs, :], wo_ref[...]) + bo_ref[...]

    kpad_ref[:, 0:win, :] = kpad_ref[:, rows:rows + win, :]
    vpad_ref[:, 0:win, :] = vpad_ref[:, rows:rows + win, :]


def _swa(x, norm_g, w_qkv16, b_qkv, sinks, w_o16, b_o):
    b, t, _ = x.shape
    assert t % SWA_ROWS == 0 and SWA_ROWS % SWA_PART == 0 and SWA_PART % SWA_WINDOW == 0
    assert SWA_ROWS % ROPE_SPAN == 0 and SWA_PART % ROPE_SPAN == 0
    half = SWA_HEAD_DIM // 2
    inv = ROPE_THETA ** (-jnp.arange(half, dtype=F32) / half)
    inv_row = jnp.tile(inv, LANES // half).reshape(1, LANES)
    n_qkv = SWA_Q + 2 * SWA_KV
    tile = pl.BlockSpec((1, SWA_ROWS, D_MODEL), lambda bi, ti, *_: (bi, ti, 0))
    pad_shape = (2 * SWA_KV_HEADS, SWA_WINDOW + SWA_ROWS, LANES)
    return pl.pallas_call(
        _swa_body,
        grid_spec=pltpu.PrefetchScalarGridSpec(
            num_scalar_prefetch=1,
            grid=(b, t // SWA_ROWS),
            in_specs=[tile, _resident((1, D_MODEL)), _resident((D_MODEL, n_qkv)), _resident((1, n_qkv)),
                      _resident((1, LANES)), _resident((SWA_Q, D_MODEL)), _resident((1, D_MODEL))],
            out_specs=tile,
            scratch_shapes=[pltpu.VMEM(pad_shape, BF16), pltpu.VMEM(pad_shape, BF16),
                            pltpu.VMEM((SWA_ROWS, SWA_Q), BF16),
                            pltpu.VMEM((ROPE_SPAN, LANES), F32), pltpu.VMEM((ROPE_SPAN, LANES), F32)]),
        out_shape=jax.ShapeDtypeStruct(x.shape, F32),
        compiler_params=pltpu.CompilerParams(dimension_semantics=("arbitrary", "arbitrary"),
                                             vmem_limit_bytes=VMEM_LIMIT_BYTES),
        name="swa_mixer",
    )(sinks.astype(F32), x, norm_g, w_qkv16, b_qkv.reshape(1, n_qkv).astype(F32), inv_row,
      w_o16, b_o.reshape(1, D_MODEL).astype(F32))


def kernel(x, ffn1_norm, ffn1_w_gate, ffn1_w_up, ffn1_w_down, mix_norm, ffn2_norm, ffn2_w_gate, ffn2_w_up,
           ffn2_w_down, hyb_w_in, gdn_conv_w, gdn_a_log, gdn_dt_bias, gdn_out_norm, sc_conv_w, hyb_w_out,
           swa_w_qkv, swa_b_qkv, swa_sinks, swa_w_o, swa_b_o, final_norm):
    b, t, d = x.shape
    depth = ffn1_norm.shape[0]
    final_g = final_norm.reshape(1, d).astype(F32)
    ffn1 = [w.astype(F32) for w in (ffn1_w_gate, ffn1_w_up, ffn1_w_down)]
    ffn2 = [w.astype(F32) for w in (ffn2_w_gate, ffn2_w_up, ffn2_w_down)]
    hyb_in16, hyb_out16 = hyb_w_in.astype(BF16), hyb_w_out.astype(BF16)
    swa_qkv16, swa_o16 = swa_w_qkv.astype(BF16), swa_w_o.astype(BF16)

    def ffn(xin, norm, ws, li, last):
        y = _ffn(xin.reshape(b * t, d), norm.reshape(1, d).astype(F32), ws[0], ws[1], ws[2], li, final_g, last)
        return y.reshape(b, t, d)

    for li in range(depth):
        x = ffn(x, ffn1_norm[li], ffn1, li, False)
        norm = mix_norm[li].reshape(1, d).astype(F32)
        j = li // 2
        if li % 2 == 0:
            x = _hybrid(x, norm, hyb_in16[j], gdn_conv_w[j], gdn_a_log[j], gdn_dt_bias[j], gdn_out_norm[j],
                        sc_conv_w[j], hyb_out16[j])
        else:
            x = _swa(x, norm, swa_qkv16[j], swa_b_qkv[j], swa_sinks[j], swa_o16[j], swa_b_o[j])
        x = ffn(x, ffn2_norm[li], ffn2, li, li == depth - 1)
    return x
```
